```python
import math
import jax
import jax.numpy as jnp
from jax import lax
import numpy as np

D_MODEL = 2048
BATCH = 4
SEQ = 4096
DEPTH = 4

D_MIX = D_MODEL
RET_HEADS = 4
RET_DK = D_MIX // 16
RET_DV = 2 * RET_DK
RET_CHUNK = 128
ROPE_BASE = 10000.0
S5_WIDTH = D_MIX // 4
S5_GROUP = 16
S5_GROUPS = S5_WIDTH // S5_GROUP
S5_STATE = 64
GLA_HEADS = 4
GLA_DV = (D_MIX // 4) // GLA_HEADS
GLA_DK = GLA_DV // 2
GLA_RANK = 16
GLA_TAU = 16.0
GLA_CHUNK = 64
D_FF = 256 * ((8 * D_MODEL // 3 + 255) // 256)
FFN_RES = 0.5
N_MOD = 9
EPS = 1e-6
IN_WIDTHS = (RET_HEADS * RET_DK, RET_HEADS * RET_DK, RET_HEADS * RET_DV, RET_HEADS * RET_DV, S5_WIDTH, GLA_HEADS * GLA_DK, GLA_HEADS * GLA_DK, GLA_HEADS * GLA_DV, GLA_HEADS * GLA_DV, 2 * GLA_RANK)
D_IN = sum(IN_WIDTHS)

kernel_name = "hybrid_parallel_retention_s5_gla_macaron"


def _rms_norm(x, g):
    xf = x.astype(jnp.float32)
    y = xf * lax.rsqrt(jnp.mean(xf * xf, axis=-1, keepdims=True) + EPS)
    return (y * g.astype(jnp.float32)).astype(x.dtype)


def _modulate(h, shift, scale):
    return h * (1.0 + scale[:, None, :]) + shift[:, None, :]


def _swiglu(h, w1, w3, w2):
    return (jax.nn.silu(h @ w1) * (h @ w3)) @ w2


def _head_layer_norm(o):
    mu = jnp.mean(o, axis=-1, keepdims=True)
    oc = o - mu
    return oc * lax.rsqrt(jnp.mean(oc * oc, axis=-1, keepdims=True) + EPS)


def _head_rms_norm(o):
    return o * lax.rsqrt(jnp.mean(o * o, axis=-1, keepdims=True) + EPS)


def _rotary(t, cos, sin):
    t1, t2 = jnp.split(t, 2, axis=-1)
    return jnp.concatenate([t1 * cos - t2 * sin, t1 * sin + t2 * cos], axis=-1)


def _retention_bidir(q, k, v, log_gamma):
    bsz, L, H, dk = q.shape
    dv = v.shape[-1]
    C = RET_CHUNK
    N = L // C
    q = q.reshape(bsz, N, C, H, dk) * (dk ** -0.5)
    k = k.reshape(bsz, N, C, H, dk)
    v = v.reshape(bsz, N, C, H, dv)
    idx = jnp.arange(C, dtype=jnp.float32)
    lg = log_gamma[None, :]
    dist = jnp.abs(idx[:, None] - idx[None, :])
    intra_decay = jnp.exp(dist[None] * log_gamma[:, None, None])
    scores = jnp.einsum('bnihd,bnjhd->bnhij', q, k) * intra_decay
    o = jnp.einsum('bnhij,bnjhe->bnihe', scores, v)
    k_fwd = k * jnp.exp((C - 1.0 - idx)[:, None] * lg)[:, :, None]
    k_bwd = k * jnp.exp(idx[:, None] * lg)[:, :, None]
    kv_fwd = jnp.einsum('bnjhd,bnjhe->nbhde', k_fwd, v)
    kv_bwd = jnp.einsum('bnjhd,bnjhe->nbhde', k_bwd, v)
    gamma_c = jnp.exp(C * log_gamma)[:, None, None]

    def step(s, kv):
        return gamma_c * s + kv, s

    s0 = jnp.zeros((bsz, H, dk, dv), q.dtype)
    _, s_fwd = lax.scan(step, s0, kv_fwd)
    _, s_bwd = lax.scan(step, s0, kv_bwd, reverse=True)
    q_fwd = q * jnp.exp((idx + 1.0)[:, None] * lg)[:, :, None]
    q_bwd = q * jnp.exp((C - idx)[:, None] * lg)[:, :, None]
    o = o + jnp.einsum('bnihd,nbhde->bnihe', q_fwd, s_fwd) + jnp.einsum('bnihd,nbhde->bnihe', q_bwd, s_bwd)
    return o.reshape(bsz, L, H, dv)


def _gla_causal(q, k, v, log_a, include_diag):
    bsz, L, H, dk = q.shape
    dv = v.shape[-1]
    C = GLA_CHUNK
    N = L // C
    q = q.reshape(bsz, N, C, H, dk)
    k = k.reshape(bsz, N, C, H, dk)
    log_a = log_a.reshape(bsz, N, C, H, dk)
    v = v.reshape(bsz, N, C, H, dv)
    b = jnp.cumsum(log_a, axis=2)
    b_last = b[:, :, -1]
    q_in = q * jnp.exp(b)
    k_in = k * jnp.exp(-b)
    scores = jnp.einsum('bnihd,bnjhd->bnhij', q_in, k_in)
    mask = jnp.tril(jnp.ones((C, C), dtype=bool), 0 if include_diag else -1)
    scores = jnp.where(mask, scores, 0.0)
    o = jnp.einsum('bnhij,bnjhe->bnihe', scores, v)
    k_st = k * jnp.exp(b_last[:, :, None] - b)
    kv = jnp.einsum('bnjhd,bnjhe->nbhde', k_st, v)
    decay = jnp.moveaxis(jnp.exp(b_last), 1, 0)[..., None]

    def step(s, inp):
        kv_n, a_n = inp
        return a_n * s + kv_n, s

    _, s_prev = lax.scan(step, jnp.zeros((bsz, H, dk, dv), q.dtype), (kv, decay))
    o = o + jnp.einsum('bnihd,nbhde->bnihe', q_in, s_prev)
    return o.reshape(bsz, L, H, dv)


def _gla_bidir(q, k, v, la_f, la_b):
    flip = lambda t: jnp.flip(t, axis=1)
    o_f = _gla_causal(q, k, v, la_f, True)
    o_b = flip(_gla_causal(flip(q), flip(k), flip(v), flip(la_b), False))
    return o_f + o_b


def _s5_bidir(u, lam_re, lam_im, log_dt, b_re, b_im, c_re, c_im, d_skip):
    f32 = jnp.float32
    bsz, L, W = u.shape
    ug = jnp.moveaxis(u.reshape(bsz, L, S5_GROUPS, S5_GROUP), 1, 0)
    lam = lax.complex(lam_re.astype(f32), lam_im.astype(f32))
    lam_bar = jnp.exp(lam * jnp.exp(log_dt.astype(f32))[..., None])
    b_bar = ((lam_bar - 1.0) / lam)[..., None] * lax.complex(b_re.astype(f32), b_im.astype(f32))
    c_mat = lax.complex(c_re.astype(f32), c_im.astype(f32))

    def combine(e1, e2):
        a1, x1 = e1
        a2, x2 = e2
        return a1 * a2, a2[:, None] * x1 + x2

    y = u * d_skip.astype(f32)
    for direction in range(2):
        bu = jnp.einsum('lbgc,gpc->lbgp', ug, b_bar[direction])
        a = jnp.broadcast_to(lam_bar[direction], (L,) + lam_bar.shape[1:])
        _, states = lax.associative_scan(combine, (a, bu), reverse=(direction == 1), axis=0)
        out = jnp.real(jnp.einsum('gcp,lbgp->lbgc', c_mat[direction], states))
        y = y + jnp.moveaxis(out, 0, 1).reshape(bsz, L, W)
    return y


def _hybrid_mixer(h, w_in, lam_re, lam_im, log_dt, b_re, b_im, c_re, c_im, s5_d, w_glu, b_glu, w_gate, b_gate, w_out):
    f32 = jnp.float32
    bsz, L, _ = h.shape
    pts, acc = [], 0
    for w in IN_WIDTHS[:-1]:
        acc += w
        pts.append(acc)
    proj = (h @ w_in).astype(f32)
    rq, rk, rv, rg, su, gq, gk, gv, gr, glr = jnp.split(proj, pts, axis=-1)
    pos = jnp.arange(L, dtype=f32)
    inv_freq = ROPE_BASE ** (-jnp.arange(0, RET_DK, 2, dtype=f32) / RET_DK)
    ang = pos[:, None] * inv_freq[None, :]
    cos, sin = jnp.cos(ang)[:, None, :], jnp.sin(ang)[:, None, :]
    rq = _rotary(rq.reshape(bsz, L, RET_HEADS, RET_DK), cos, sin)
    rk = _rotary(rk.reshape(bsz, L, RET_HEADS, RET_DK), cos, sin)
    rv = rv.reshape(bsz, L, RET_HEADS, RET_DV)
    log_gamma = jnp.log1p(-jnp.exp2(-5.0 - jnp.arange(RET_HEADS, dtype=f32)))
    o_ret = _retention_bidir(rq, rk, rv, log_gamma)
    y_ret = jax.nn.silu(rg) * _head_layer_norm(o_ret).reshape(bsz, L, RET_HEADS * RET_DV)
    y_s5 = _s5_bidir(su, lam_re, lam_im, log_dt, b_re, b_im, c_re, c_im, s5_d)
    z = jax.nn.gelu(y_s5)
    y_s5 = z * jax.nn.sigmoid(z @ w_glu.astype(f32) + b_glu.astype(f32))
    lr_f, lr_b = jnp.split(glr, 2, axis=-1)
    w_gate = w_gate.astype(f32)
    b_gate = b_gate.astype(f32)
    la_f = (jax.nn.log_sigmoid(lr_f @ w_gate[0] + b_gate[0]) / GLA_TAU).reshape(bsz, L, GLA_HEADS, GLA_DK)
    la_b = (jax.nn.log_sigmoid(lr_b @ w_gate[1] + b_gate[1]) / GLA_TAU).reshape(bsz, L, GLA_HEADS, GLA_DK)
    gq = gq.reshape(bsz, L, GLA_HEADS, GLA_DK) * (GLA_DK ** -0.5)
    gk = gk.reshape(bsz, L, GLA_HEADS, GLA_DK)
    gv = gv.reshape(bsz, L, GLA_HEADS, GLA_DV)
    o_gla = _gla_bidir(gq, gk, gv, la_f, la_b)
    y_gla = jax.nn.silu(gr) * _head_rms_norm(o_gla).reshape(bsz, L, GLA_HEADS * GLA_DV)
    y = jnp.concatenate([y_ret, y_s5, y_gla], axis=-1).astype(h.dtype)
    return y @ w_out


def setup_inputs(seed: int = 0) -> dict:
    key = jax.random.key(seed)
    ks = jax.random.split(key, 28)
    f32 = jnp.float32

    def nrm(k, shape, s):
        return jax.random.normal(k, shape, f32) * s

    G, P, Cg = S5_GROUPS, S5_STATE, S5_GROUP
    x = nrm(ks[0], (BATCH, SEQ, D_MODEL), 1.0)
    c = nrm(ks[1], (BATCH, D_MODEL), 1.0)
    w_ada = nrm(ks[2], (DEPTH, D_MODEL, N_MOD * D_MODEL), 0.5 * D_MODEL ** -0.5)
    b_ada = nrm(ks[3], (DEPTH, N_MOD * D_MODEL), 0.01)
    g_ffn1 = 1.0 + nrm(ks[4], (DEPTH, D_MODEL), 0.01)
    ffn1_w1 = nrm(ks[5], (DEPTH, D_MODEL, D_FF), D_MODEL ** -0.5)
    ffn1_w3 = nrm(ks[6], (DEPTH, D_MODEL, D_FF), D_MODEL ** -0.5)
    ffn1_w2 = nrm(ks[7], (DEPTH, D_FF, D_MODEL), D_FF ** -0.5)
    g_mix = 1.0 + nrm(ks[8], (DEPTH, D_MODEL), 0.01)
    w_in = nrm(ks[9], (DEPTH, D_MODEL, D_IN), D_MODEL ** -0.5)
    s5_lam_re = -0.5 + nrm(ks[10], (DEPTH, 2, G, P), 0.01)
    s5_lam_im = math.pi * jnp.arange(P, dtype=f32) + nrm(ks[11], (DEPTH, 2, G, P), 0.01)
    s5_log_dt = jax.random.uniform(ks[12], (DEPTH, 2, G), f32, math.log(1e-3), math.log(1e-1))
    s5_b_re = nrm(ks[13], (DEPTH, 2, G, P, Cg), (2 * Cg) ** -0.5)
    s5_b_im = nrm(ks[14], (DEPTH, 2, G, P, Cg), (2 * Cg) ** -0.5)
    s5_c_re = nrm(ks[15], (DEPTH, 2, G, Cg, P), P ** -0.5)
    s5_c_im = nrm(ks[16], (DEPTH, 2, G, Cg, P), P ** -0.5)
    s5_d = nrm(ks[17], (DEPTH, S5_WIDTH), 1.0)
    s5_w_glu = nrm(ks[18], (DEPTH, S5_WIDTH, S5_WIDTH), S5_WIDTH ** -0.5)
    s5_b_glu = nrm(ks[19], (DEPTH, S5_WIDTH), 0.01)
    gla_w_gate = nrm(ks[20], (DEPTH, 2, GLA_RANK, GLA_HEADS * GLA_DK), GLA_RANK ** -0.5)
    gla_b_gate = nrm(ks[21], (DEPTH, 2, GLA_HEADS * GLA_DK), 0.01)
    w_out = nrm(ks[22], (DEPTH, D_MIX, D_MODEL), D_MIX ** -0.5)
    g_ffn2 = 1.0 + nrm(ks[23], (DEPTH, D_MODEL), 0.01)
    ffn2_w1 = nrm(ks[24], (DEPTH, D_MODEL, D_FF), D_MODEL ** -0.5)
    ffn2_w3 = nrm(ks[25], (DEPTH, D_MODEL, D_FF), D_MODEL ** -0.5)
    ffn2_w2 = nrm(ks[26], (DEPTH, D_FF, D_MODEL), D_FF ** -0.5)
    g_final = 1.0 + nrm(ks[27], (D_MODEL,), 0.01)
    return {"x": x, "c": c, "w_ada": w_ada, "b_ada": b_ada, "g_ffn1": g_ffn1, "ffn1_w1": ffn1_w1, "ffn1_w3": ffn1_w3, "ffn1_w2": ffn1_w2, "g_mix": g_mix, "w_in": w_in, "s5_lam_re": s5_lam_re, "s5_lam_im": s5_lam_im, "s5_log_dt": s5_log_dt, "s5_b_re": s5_b_re, "s5_b_im": s5_b_im, "s5_c_re": s5_c_re, "s5_c_im": s5_c_im, "s5_d": s5_d, "s5_w_glu": s5_w_glu, "s5_b_glu": s5_b_glu, "gla_w_gate": gla_w_gate, "gla_b_gate": gla_b_gate, "w_out": w_out, "g_ffn2": g_ffn2, "ffn2_w1": ffn2_w1, "ffn2_w3": ffn2_w3, "ffn2_w2": ffn2_w2, "g_final": g_final}


def reference(x, c, w_ada, b_ada, g_ffn1, ffn1_w1, ffn1_w3, ffn1_w2, g_mix, w_in, s5_lam_re, s5_lam_im, s5_log_dt, s5_b_re, s5_b_im, s5_c_re, s5_c_im, s5_d, s5_w_glu, s5_b_glu, gla_w_gate, gla_b_gate, w_out, g_ffn2, ffn2_w1, ffn2_w3, ffn2_w2, g_final):
    h = x
    cond = jax.nn.silu(c)
    for l in range(DEPTH):
        mod = cond @ w_ada[l] + b_ada[l]
        sh1, sc1, gt1, sh2, sc2, gt2, sh3, sc3, gt3 = jnp.split(mod, N_MOD, axis=-1)
        u = _modulate(_rms_norm(h, g_ffn1[l]), sh1, sc1)
        h = h + FFN_RES * gt1[:, None, :] * _swiglu(u, ffn1_w1[l], ffn1_w3[l], ffn1_w2[l])
        u = _modulate(_rms_norm(h, g_mix[l]), sh2, sc2)
        h = h + gt2[:, None, :] * _hybrid_mixer(u, w_in[l], s5_lam_re[l], s5_lam_im[l], s5_log_dt[l], s5_b_re[l], s5_b_im[l], s5_c_re[l], s5_c_im[l], s5_d[l], s5_w_glu[l], s5_b_glu[l], gla_w_gate[l], gla_b_gate[l], w_out[l])
        u = _modulate(_rms_norm(h, g_ffn2[l]), sh3, sc3)
        h = h + FFN_RES * gt3[:, None, :] * _swiglu(u, ffn2_w1[l], ffn2_w3[l], ffn2_w2[l])
    return _rms_norm(h, g_final)
```

```python
import functools
import math

import jax
import jax.numpy as jnp
from jax import lax
from jax.experimental import pallas as pl
from jax.experimental.pallas import tpu as pltpu

F32 = jnp.float32
BF16 = jnp.bfloat16
HIGHEST = lax.Precision.HIGHEST

RET_HEADS = 4
RET_DK = 128
RET_DV = 256
RET_CHUNK = 128
ROPE_BASE = 10000.0
S5_GROUP = 16
S5_STATE = 64
S5_CHUNK = 32
GLA_HEADS = 4
GLA_DK = 64
GLA_DV = 128
GLA_RANK = 16
GLA_TAU = 16.0
GLA_CHUNK = 64
FFN_RES = 0.5
N_MOD = 9
EPS = 1e-6

LANES = 128
VMEM_LIMIT = 56 << 20

NT_DIMS = (((1,), (1,)), ((), ()))
TN_DIMS = (((0,), (0,)), ((), ()))


def _params(*semantics):
    return pltpu.CompilerParams(dimension_semantics=semantics, vmem_limit_bytes=VMEM_LIMIT)


def _sigmoid(x):
    return 1.0 / (1.0 + jnp.exp(-x))


def _norm_mod(x, g, shift, scale):
    ms = jnp.mean(x * x, axis=-1, keepdims=True)
    return (x * lax.rsqrt(ms + EPS) * g) * (1.0 + scale) + shift


def _ada_kernel(c_ref, w_ref, b_ref, o_ref):
    c = c_ref[...]
    cond = c * _sigmoid(c)
    o_ref[0] = jnp.dot(cond, w_ref[0], preferred_element_type=F32, precision=HIGHEST) + b_ref[0]


def _ada_mod(c, w_ada, b_ada):
    depth, d, n = w_ada.shape
    bsz = c.shape[0]
    rows = -(-bsz // 8) * 8
    c_pad = jnp.zeros((rows, d), F32).at[:bsz].set(c)
    tn = n // 16
    out = pl.pallas_call(
        _ada_kernel,
        grid=(depth, n // tn),
        in_specs=[
            pl.BlockSpec((rows, d), lambda l, j: (0, 0)),
            pl.BlockSpec((1, d, tn), lambda l, j: (l, 0, j)),
            pl.BlockSpec((1, 1, tn), lambda l, j: (l, 0, j)),
        ],
        out_specs=pl.BlockSpec((1, rows, tn), lambda l, j: (l, 0, j)),
        out_shape=jax.ShapeDtypeStruct((depth, rows, n), F32),
        compiler_params=_params("parallel", "parallel"),
        name="ada_mod",
    )(c_pad, w_ada, b_ada.reshape(depth, 1, n))
    return out[:, :bsz]


def _ffn_kernel(h_ref, g_ref, sh_ref, sc_ref, gt_ref, w13_ref, w2_ref, *rest, tf, nj, final):
    if final:
        gf_ref, o_ref, u_scr, acc_scr = rest
    else:
        o_ref, u_scr, acc_scr = rest
    j = pl.program_id(1)

    @pl.when(j == 0)
    def _():
        u_scr[...] = _norm_mod(h_ref[...], g_ref[...], sh_ref[0], sc_ref[0]).astype(BF16)

    r = jnp.dot(u_scr[...], w13_ref[0], preferred_element_type=F32)
    a = r[:, :tf]
    act = (a * _sigmoid(a) * r[:, tf:]).astype(BF16)
    contrib = jnp.dot(act, w2_ref[...], preferred_element_type=F32)

    @pl.when(j == 0)
    def _():
        acc_scr[...] = contrib

    @pl.when(j > 0)
    def _():
        acc_scr[...] += contrib

    @pl.when(j == nj - 1)
    def _():
        out = h_ref[...] + (FFN_RES * gt_ref[0]) * acc_scr[...]
        if final:
            ms = jnp.mean(out * out, axis=-1, keepdims=True)
            out = out * lax.rsqrt(ms + EPS) * gf_ref[...]
        o_ref[...] = out


def _ffn(h, seq_len, g, shift, scale, gate, w1, w3, w2, g_final=None, *, tm=512, tf=512):
    m, d = h.shape
    dff = w1.shape[1]
    tm = min(tm, seq_len)
    nj = dff // tf
    per_batch = seq_len // tm
    w13 = jnp.concatenate(
        [w1.astype(BF16).reshape(d, nj, tf), w3.astype(BF16).reshape(d, nj, tf)], axis=2
    ).transpose(1, 0, 2)
    w2b = w2.astype(BF16)
    final = g_final is not None
    vec = lambda i, j: (i // per_batch, 0, 0)
    in_specs = [
        pl.BlockSpec((tm, d), lambda i, j: (i, 0)),
        pl.BlockSpec((1, d), lambda i, j: (0, 0)),
        pl.BlockSpec((1, 1, d), vec),
        pl.BlockSpec((1, 1, d), vec),
        pl.BlockSpec((1, 1, d), vec),
        pl.BlockSpec((1, d, 2 * tf), lambda i, j: (j, 0, 0)),
        pl.BlockSpec((tf, d), lambda i, j: (j, 0)),
    ]
    args = [h, g.reshape(1, d), shift, scale, gate, w13, w2b]
    if final:
        in_specs.append(pl.BlockSpec((1, d), lambda i, j: (0, 0)))
        args.append(g_final.reshape(1, d))
    return pl.pallas_call(
        functools.partial(_ffn_kernel, tf=tf, nj=nj, final=final),
        grid=(m // tm, nj),
        in_specs=in_specs,
        out_specs=pl.BlockSpec((tm, d), lambda i, j: (i, 0)),
        out_shape=jax.ShapeDtypeStruct((m, d), F32),
        scratch_shapes=[pltpu.VMEM((tm, d), BF16), pltpu.VMEM((tm, d), F32)],
        compiler_params=_params("parallel", "arbitrary"),
        name="ffn",
    )(*args)


def _inproj_kernel(h_ref, g_ref, sh_ref, sc_ref, w_ref, o_ref):
    u = _norm_mod(h_ref[...], g_ref[...], sh_ref[0], sc_ref[0]).astype(BF16)
    o_ref[...] = jnp.dot(u, w_ref[...], preferred_element_type=F32)


def _inproj(h, seq_len, g, shift, scale, w_in, *, tm=256):
    m, d = h.shape
    d_in = w_in.shape[1]
    d_pad = -(-d_in // LANES) * LANES
    w = jnp.zeros((d, d_pad), BF16).at[:, :d_in].set(w_in.astype(BF16))
    tm = min(tm, seq_len)
    per_batch = seq_len // tm
    vec = lambda i: (i // per_batch, 0, 0)
    return pl.pallas_call(
        _inproj_kernel,
        grid=(m // tm,),
        in_specs=[
            pl.BlockSpec((tm, d), lambda i: (i, 0)),
            pl.BlockSpec((1, d), lambda i: (0, 0)),
            pl.BlockSpec((1, 1, d), vec),
            pl.BlockSpec((1, 1, d), vec),
            pl.BlockSpec((d, d_pad), lambda i: (0, 0), pipeline_mode=pl.Buffered(1)),
        ],
        out_specs=pl.BlockSpec((tm, d_pad), lambda i: (i, 0)),
        out_shape=jax.ShapeDtypeStruct((m, d_pad), F32),
        compiler_params=_params("parallel"),
        name="inproj",
    )(h, g.reshape(1, d), shift, scale, w)


def _ret_tables(seq_len):
    c = RET_CHUNK
    pos = jnp.arange(seq_len, dtype=F32)
    inv_freq = ROPE_BASE ** (-jnp.arange(0, RET_DK, 2, dtype=F32) / RET_DK)
    ang = pos[:, None] * inv_freq[None, :]
    cos, sin = jnp.cos(ang), jnp.sin(ang)
    cs = jnp.concatenate([cos, cos], axis=-1)
    sn = jnp.concatenate([-sin, sin], axis=-1)
    lg = jnp.log1p(-jnp.exp2(-5.0 - jnp.arange(RET_HEADS, dtype=F32)))
    idx = jnp.arange(c, dtype=F32)
    dist = jnp.abs(idx[:, None] - idx[None, :])
    dmat = jnp.exp(dist[None] * lg[:, None, None])
    rows = jnp.stack([c - 1.0 - idx, idx, idx + 1.0, c - idx])
    dvec = jnp.exp(rows[None] * lg[:, None, None])
    dvec = jnp.broadcast_to(dvec[..., None], dvec.shape + (RET_DK,))
    gc = jnp.broadcast_to(jnp.exp(c * lg)[:, None, None], (RET_HEADS, 1, RET_DV))
    return cs, sn, dmat, dvec, gc


def _ret_kernel(q_ref, k_ref, v_ref, g_ref, cs_ref, sn_ref, dmat_ref, dvec_ref, gc_ref, o_ref,
                q_scr, k_scr, o_scr, sf_scr, sb_scr, *, nchunks):
    c = RET_CHUNK
    scale = RET_DK ** -0.5
    dmat = dmat_ref[0]
    k_fwd, k_bwd, q_fwd, q_bwd = (dvec_ref[0, i] for i in range(4))
    gc = gc_ref[0]
    sf_scr[...] = jnp.zeros_like(sf_scr)
    sb_scr[...] = jnp.zeros_like(sb_scr)

    def forward(n, carry):
        rows = pl.ds(pl.multiple_of(n * c, c), c)
        cs = cs_ref[rows, :]
        sn = sn_ref[rows, :]
        q = q_ref[0, rows, :]
        k = k_ref[0, rows, :]
        q = (q * cs + pltpu.roll(q, RET_DK // 2, 1) * sn) * scale
        k = k * cs + pltpu.roll(k, RET_DK // 2, 1) * sn
        q_scr[rows, :] = q
        k_scr[rows, :] = k
        vb = v_ref[0, rows, :].astype(BF16)
        scores = lax.dot_general(q.astype(BF16), k.astype(BF16), NT_DIMS,
                                 preferred_element_type=F32) * dmat
        o = jnp.dot(scores.astype(BF16), vb, preferred_element_type=F32)
        s = sf_scr[...]
        o = o + jnp.dot((q * q_fwd).astype(BF16), s.astype(BF16), preferred_element_type=F32)
        kv = lax.dot_general((k * k_fwd).astype(BF16), vb, TN_DIMS, preferred_element_type=F32)
        sf_scr[...] = gc * s + kv
        o_scr[rows, :] = o
        return carry

    lax.fori_loop(0, nchunks, forward, 0)

    def backward(t, carry):
        n = nchunks - 1 - t
        rows = pl.ds(pl.multiple_of(n * c, c), c)
        q = q_scr[rows, :]
        k = k_scr[rows, :]
        vb = v_ref[0, rows, :].astype(BF16)
        s = sb_scr[...]
        o = o_scr[rows, :] + jnp.dot((q * q_bwd).astype(BF16), s.astype(BF16),
                                     preferred_element_type=F32)
        kv = lax.dot_general((k * k_bwd).astype(BF16), vb, TN_DIMS, preferred_element_type=F32)
        sb_scr[...] = gc * s + kv
        oc = o - jnp.mean(o, axis=-1, keepdims=True)
        ln = oc * lax.rsqrt(jnp.mean(oc * oc, axis=-1, keepdims=True) + EPS)
        g = g_ref[0, rows, :]
        o_ref[0, rows, :] = (g * _sigmoid(g) * ln).astype(o_ref.dtype)
        return carry

    lax.fori_loop(0, nchunks, backward, 0)


def _retention(proj, tables):
    bsz, seq_len, _ = proj.shape
    cs, sn, dmat, dvec, gc = tables
    h = RET_HEADS
    kq, kk = 0, h
    kv, kg = (2 * h * RET_DK) // RET_DV, (2 * h * RET_DK + h * RET_DV) // RET_DV
    tok = lambda off: (lambda b, i: (b, 0, off + i))
    return pl.pallas_call(
        functools.partial(_ret_kernel, nchunks=seq_len // RET_CHUNK),
        grid=(bsz, h),
        in_specs=[
            pl.BlockSpec((1, seq_len, RET_DK), tok(kq)),
            pl.BlockSpec((1, seq_len, RET_DK), tok(kk)),
            pl.BlockSpec((1, seq_len, RET_DV), tok(kv)),
            pl.BlockSpec((1, seq_len, RET_DV), tok(kg)),
            pl.BlockSpec((seq_len, RET_DK), lambda b, i: (0, 0)),
            pl.BlockSpec((seq_len, RET_DK), lambda b, i: (0, 0)),
            pl.BlockSpec((1, RET_CHUNK, RET_CHUNK), lambda b, i: (i, 0, 0)),
            pl.BlockSpec((1, 4, RET_CHUNK, RET_DK), lambda b, i: (i, 0, 0, 0)),
            pl.BlockSpec((1, 1, RET_DV), lambda b, i: (i, 0, 0)),
        ],
        out_specs=pl.BlockSpec((1, seq_len, RET_DV), lambda b, i: (b, 0, i)),
        out_shape=jax.ShapeDtypeStruct((bsz, seq_len, h * RET_DV), BF16),
        scratch_shapes=[
            pltpu.VMEM((seq_len, RET_DK), F32),
            pltpu.VMEM((seq_len, RET_DK), F32),
            pltpu.VMEM((seq_len, RET_DV), F32),
            pltpu.VMEM((RET_DK, RET_DV), F32),
            pltpu.VMEM((RET_DK, RET_DV), F32),
        ],
        compiler_params=_params("parallel", "parallel"),
        name="retention",
    )(proj, proj, proj, proj, cs, sn, dmat, dvec, gc)


def _gla_kernel(q_ref, k_ref, v_ref, g_ref, lr_ref, wf_ref, wb_ref, bf_ref, bb_ref, o_ref,
                laf_scr, lab_scr, o_scr, st_scr, *, nchunks, gate_rows):
    c = GLA_CHUNK
    dk2, dv2 = 2 * GLA_DK, 2 * GLA_DV
    scale = GLA_DK ** -0.5
    seq_len = nchunks * c

    def gates(m, carry):
        rows = pl.ds(pl.multiple_of(m * gate_rows, gate_rows), gate_rows)
        lr = lr_ref[0, rows, :]
        for w_ref, b_ref, la_scr in ((wf_ref, bf_ref, laf_scr), (wb_ref, bb_ref, lab_scr)):
            x = jnp.dot(lr, w_ref[0], preferred_element_type=F32, precision=HIGHEST) + b_ref[0]
            log_sig = jnp.minimum(x, 0.0) - jnp.log(1.0 + jnp.exp(-jnp.abs(x)))
            la_scr[rows, :] = log_sig * (1.0 / GLA_TAU)
        return carry

    lax.fori_loop(0, seq_len // gate_rows, gates, 0)

    lane = lax.broadcasted_iota(jnp.int32, (c, dk2), 1)
    ri = lax.broadcasted_iota(jnp.int32, (c, c), 0)
    ci = lax.broadcasted_iota(jnp.int32, (c, c), 1)
    lower = jnp.where(ci <= ri, 1.0, 0.0).astype(F32)
    upper = jnp.where(ci >= ri, 1.0, 0.0).astype(F32)
    se = lax.broadcasted_iota(jnp.int32, (dv2, dk2), 0)
    sd = lax.broadcasted_iota(jnp.int32, (dv2, dk2), 1)
    same_head = jnp.where(se < GLA_DV, jnp.where(sd < GLA_DK, 1.0, 0.0),
                          jnp.where(sd >= GLA_DK, 1.0, 0.0)).astype(F32)

    def chunk(n, la_scr, reverse):
        rows = pl.ds(pl.multiple_of(n * c, c), c)
        la = la_scr[rows, :]
        if reverse:
            b = jnp.dot(upper, la, preferred_element_type=F32, precision=HIGHEST)
            total = b[0:1, :]
            keep = ci > ri
        else:
            b = jnp.dot(lower, la, preferred_element_type=F32, precision=HIGHEST)
            total = b[c - 1:c, :]
            keep = ci <= ri
        q = q_ref[0, rows, :] * scale
        k = k_ref[0, rows, :]
        q_in = q * jnp.exp(b)
        k_in = (k * jnp.exp(-b)).astype(BF16)
        k_st = (k * jnp.exp(total - b)).astype(BF16)
        vb = v_ref[0, rows, :].astype(BF16)
        outs = []
        for hh in range(2):
            mine = (lane < GLA_DK) if hh == 0 else (lane >= GLA_DK)
            qm = jnp.where(mine, q_in, 0.0).astype(BF16)
            scores = lax.dot_general(qm, k_in, NT_DIMS, preferred_element_type=F32)
            scores = jnp.where(keep, scores, 0.0).astype(BF16)
            outs.append(jnp.dot(scores, vb[:, hh * GLA_DV:(hh + 1) * GLA_DV],
                                preferred_element_type=F32))
        o = jnp.concatenate(outs, axis=1)
        st = st_scr[...]
        o = o + lax.dot_general(q_in.astype(BF16), st.astype(BF16), NT_DIMS,
                                preferred_element_type=F32)
        kv_t = lax.dot_general(vb, k_st, TN_DIMS, preferred_element_type=F32)
        st_scr[...] = st * jnp.exp(total) + kv_t * same_head
        return rows, o

    st_scr[...] = jnp.zeros_like(st_scr)

    def forward(n, carry):
        rows, o = chunk(n, laf_scr, False)
        o_scr[rows, :] = o
        return carry

    lax.fori_loop(0, nchunks, forward, 0)
    st_scr[...] = jnp.zeros_like(st_scr)

    def backward(t, carry):
        rows, o = chunk(nchunks - 1 - t, lab_scr, True)
        o = o + o_scr[rows, :]
        normed = []
        for hh in range(2):
            oh = o[:, hh * GLA_DV:(hh + 1) * GLA_DV]
            normed.append(oh * lax.rsqrt(jnp.mean(oh * oh, axis=-1, keepdims=True) + EPS))
        g = g_ref[0, rows, :]
        o_ref[0, rows, :] = (g * _sigmoid(g) * jnp.concatenate(normed, axis=1)).astype(o_ref.dtype)
        return carry

    lax.fori_loop(0, nchunks, backward, 0)


def _gla(proj, w_gate, b_gate, col0):
    bsz, seq_len, _ = proj.shape
    pairs = GLA_HEADS // 2
    dk2, dv2 = 2 * GLA_DK, 2 * GLA_DV
    c_q = col0
    c_k = c_q + GLA_HEADS * GLA_DK
    c_v = c_k + GLA_HEADS * GLA_DK
    c_g = c_v + GLA_HEADS * GLA_DV
    c_lr = c_g + GLA_HEADS * GLA_DV
    wg = w_gate.astype(F32).reshape(2, GLA_RANK, pairs, dk2).transpose(0, 2, 1, 3)
    wf = jnp.zeros((pairs, LANES, dk2), F32).at[:, :GLA_RANK].set(wg[0])
    wb = jnp.zeros((pairs, LANES, dk2), F32).at[:, GLA_RANK:2 * GLA_RANK].set(wg[1])
    bg = b_gate.astype(F32).reshape(2, pairs, 1, dk2)
    tok = lambda col, width: (lambda b, p: (b, 0, col // width + p))
    mat = lambda b, p: (p, 0, 0)
    gate_rows = min(256, seq_len)
    return pl.pallas_call(
        functools.partial(_gla_kernel, nchunks=seq_len // GLA_CHUNK, gate_rows=gate_rows),
        grid=(bsz, pairs),
        in_specs=[
            pl.BlockSpec((1, seq_len, dk2), tok(c_q, dk2)),
            pl.BlockSpec((1, seq_len, dk2), tok(c_k, dk2)),
            pl.BlockSpec((1, seq_len, dv2), tok(c_v, dv2)),
            pl.BlockSpec((1, seq_len, dv2), tok(c_g, dv2)),
            pl.BlockSpec((1, seq_len, LANES), lambda b, p: (b, 0, c_lr // LANES)),
            pl.BlockSpec((1, LANES, dk2), mat),
            pl.BlockSpec((1, LANES, dk2), mat),
            pl.BlockSpec((1, 1, dk2), mat),
            pl.BlockSpec((1, 1, dk2), mat),
        ],
        out_specs=pl.BlockSpec((1, seq_len, dv2), lambda b, p: (b, 0, p)),
        out_shape=jax.ShapeDtypeStruct((bsz, seq_len, GLA_HEADS * GLA_DV), BF16),
        scratch_shapes=[
            pltpu.VMEM((seq_len, dk2), F32),
            pltpu.VMEM((seq_len, dk2), F32),
            pltpu.VMEM((seq_len, dv2), F32),
            pltpu.VMEM((dv2, dk2), F32),
        ],
        compiler_params=_params("parallel", "parallel"),
        name="gla",
    )(proj, proj, proj, proj, proj, wf, wb, bg[0], bg[1])


def _swap(x):
    return pltpu.roll(x, S5_STATE, 1)


def _s5_prep_kernel(lre_ref, lim_ref, ldt_ref, brr_ref, bis_ref, cr_ref, ci_ref,
                    m_ref, e_ref, ft_ref, a_ref, *, nsteps):
    t_len = S5_CHUNK
    gsz = S5_GROUP
    width = gsz * t_len
    tp = t_len + 8
    first = lax.broadcasted_iota(jnp.int32, (1, LANES), 1) < S5_STATE
    one_zero = jnp.where(first, 1.0, 0.0).astype(F32)
    sign = jnp.where(first, 1.0, -1.0).astype(F32)

    def packed_exp(t, are, aim):
        ph = t * aim
        return jnp.exp(t * are) * jnp.where(first, jnp.cos(ph), jnp.sin(ph))

    def expand(table, tsel, nrows):
        r = lax.broadcasted_iota(jnp.int32, (nrows, tp), 0)
        tcol = lax.broadcasted_iota(jnp.int32, (nrows, tp), 1)
        sel = jnp.where(tsel(r // gsz) == tcol, 1.0, 0.0).astype(F32)
        return jnp.dot(sel, table, preferred_element_type=F32, precision=HIGHEST)

    def dup_re(x):
        return jnp.where(first, x, _swap(x))

    def dup_im_signed(x):
        return jnp.where(first, -_swap(x), x)

    panels = []
    for d in range(2):
        lre = lre_ref[0, d:d + 1, :]
        lim = lim_ref[0, d:d + 1, :]
        dt = jnp.exp(ldt_ref[0, d:d + 1, :])
        are, aim = lre * dt, lim * dt
        tcol = lax.broadcasted_iota(jnp.int32, (tp, 1), 0).astype(F32)
        pw = packed_exp(tcol, are, aim)
        num = pw[1:2, :] - one_zero
        inv_den = 1.0 / (lre * lre + lim * lim)
        coef = num * (lre * inv_den) + _swap(num) * (lim * inv_den * sign)
        bbar = coef * brr_ref[0, d] + _swap(coef) * bis_ref[0, d]
        tsel_e = (lambda j: t_len - 1 - j) if d == 0 else (lambda j: j)
        pe = expand(pw, tsel_e, width)
        e_mat = (pe * jnp.tile(dup_re(bbar), (t_len, 1))
                 + _swap(pe) * jnp.tile(dup_im_signed(bbar), (t_len, 1)))
        e_ref[0, :, d * LANES:(d + 1) * LANES] = e_mat.astype(BF16)
        tsel_g = (lambda s: s) if d == 0 else (lambda s: t_len - s)
        pg = expand(pw, tsel_g, width + gsz)
        g_mat = (pg * jnp.tile(cr_ref[0, d], (t_len + 1, 1))
                 + _swap(pg) * jnp.tile(ci_ref[0, d], (t_len + 1, 1)))
        if d == 0:
            ft_ref[0, d] = g_mat[gsz:, :].astype(BF16)
            g_panel = g_mat[:width, :]
        else:
            ft_ref[0, d] = g_mat[:width, :].astype(BF16)
            g_panel = g_mat[gsz:, :]
        panels.append(lax.dot_general(bbar, g_panel, NT_DIMS, preferred_element_type=F32,
                                      precision=HIGHEST))
        kcol = lax.broadcasted_iota(jnp.int32, (8, 1), 0)
        tk = (t_len * jnp.left_shift(1, kcol)).astype(F32)
        ak = packed_exp(tk, are, aim)
        a_ref[0, d, 0:8, :] = dup_re(ak)
        a_ref[0, d, 8:16, :] = dup_im_signed(ak)

    p_fwd, p_bwd = panels
    lane = lax.broadcasted_iota(jnp.int32, (gsz, width), 1)
    for j in range(t_len):
        blk = jnp.where(lane >= gsz * j, pltpu.roll(p_fwd, gsz * j, 1) if j else p_fwd, 0.0)
        shift = (gsz * (j + 1)) % width
        blk = blk + jnp.where(lane < gsz * (j + 1), pltpu.roll(p_bwd, shift, 1) if shift else p_bwd, 0.0)
        m_ref[0, gsz * j:gsz * (j + 1), :] = blk.astype(BF16)


def _s5_apply_kernel(u_ref, m_ref, e_ref, ft_ref, a_ref, d_ref, y_ref, *, nchunks, nsteps):
    u = u_ref[0]
    ub = u.astype(BF16)
    rows = u.shape[0]
    y = jnp.dot(ub, m_ref[0], preferred_element_type=F32) + u * d_ref[0]
    s = jnp.dot(ub, e_ref[0], preferred_element_type=F32)
    nidx = lax.broadcasted_iota(jnp.int32, (rows, LANES), 0) % nchunks
    for d in range(2):
        x = s[:, d * LANES:(d + 1) * LANES]

        def shifted(val, sh):
            if d == 0:
                return jnp.where(nidx >= sh, pltpu.roll(val, sh, 0), 0.0)
            return jnp.where(nidx < nchunks - sh, pltpu.roll(val, rows - sh, 0), 0.0)

        for k in range(nsteps):
            xs = shifted(x, 1 << k)
            x = x + a_ref[0, d, k:k + 1, :] * xs + a_ref[0, d, 8 + k:9 + k, :] * _swap(xs)
        xp = shifted(x, 1).astype(BF16)
        y = y + lax.dot_general(xp, ft_ref[0, d], NT_DIMS, preferred_element_type=F32)
    y_ref[0] = y


def _s5(su, lam_re, lam_im, log_dt, b_re, b_im, c_re, c_im, d_skip):
    bsz, seq_len, w = su.shape
    g = w // S5_GROUP
    t_len = S5_CHUNK
    nchunks = seq_len // t_len
    nsteps = max(1, (nchunks - 1).bit_length())
    rows = bsz * nchunks
    width = S5_GROUP * t_len
    dup = lambda a: jnp.concatenate([a, a], axis=-1)
    gd = lambda a: jnp.moveaxis(a.astype(F32), 0, 1)
    lre = dup(gd(lam_re))
    lim = dup(gd(lam_im))
    ldt = jnp.broadcast_to(gd(log_dt)[..., None], (g, 2, LANES))
    bt_re = jnp.swapaxes(gd(b_re), -1, -2)
    bt_im = jnp.swapaxes(gd(b_im), -1, -2)
    brr = jnp.concatenate([bt_re, bt_re], axis=-1)
    bis = jnp.concatenate([-bt_im, bt_im], axis=-1)
    cre, cim = gd(c_re), gd(c_im)
    cr = jnp.concatenate([cre, -cre], axis=-1)
    ci = jnp.concatenate([-cim, -cim], axis=-1)
    vec3 = pl.BlockSpec((1, 2, LANES), lambda i: (i, 0, 0))
    mat4 = pl.BlockSpec((1, 2, S5_GROUP, LANES), lambda i: (i, 0, 0, 0))
    m_mat, e_mat, ft_mat, a_mat = pl.pallas_call(
        functools.partial(_s5_prep_kernel, nsteps=nsteps),
        grid=(g,),
        in_specs=[vec3, vec3, vec3, mat4, mat4, mat4, mat4],
        out_specs=[
            pl.BlockSpec((1, width, width), lambda i: (i, 0, 0)),
            pl.BlockSpec((1, width, 2 * LANES), lambda i: (i, 0, 0)),
            pl.BlockSpec((1, 2, width, LANES), lambda i: (i, 0, 0, 0)),
            pl.BlockSpec((1, 2, 16, LANES), lambda i: (i, 0, 0, 0)),
        ],
        out_shape=[
            jax.ShapeDtypeStruct((g, width, width), BF16),
            jax.ShapeDtypeStruct((g, width, 2 * LANES), BF16),
            jax.ShapeDtypeStruct((g, 2, width, LANES), BF16),
            jax.ShapeDtypeStruct((g, 2, 16, LANES), F32),
        ],
        compiler_params=_params("parallel"),
        name="s5_prep",
    )(lre, lim, ldt, brr, bis, cr, ci)
    ug = su.reshape(bsz, nchunks, t_len, g, S5_GROUP).transpose(3, 0, 1, 2, 4).reshape(g, rows, width)
    d_tiled = jnp.tile(d_skip.astype(F32).reshape(g, 1, S5_GROUP), (1, 1, t_len))
    yg = pl.pallas_call(
        functools.partial(_s5_apply_kernel, nchunks=nchunks, nsteps=nsteps),
        grid=(g,),
        in_specs=[
            pl.BlockSpec((1, rows, width), lambda i: (i, 0, 0)),
            pl.BlockSpec((1, width, width), lambda i: (i, 0, 0)),
            pl.BlockSpec((1, width, 2 * LANES), lambda i: (i, 0, 0)),
            pl.BlockSpec((1, 2, width, LANES), lambda i: (i, 0, 0, 0)),
            pl.BlockSpec((1, 2, 16, LANES), lambda i: (i, 0, 0, 0)),
            pl.BlockSpec((1, 1, width), lambda i: (i, 0, 0)),
        ],
        out_specs=pl.BlockSpec((1, rows, width), lambda i: (i, 0, 0)),
        out_shape=jax.ShapeDtypeStruct((g, rows, width), F32),
        compiler_params=_params("parallel"),
        name="s5_apply",
    )(ug, m_mat, e_mat, ft_mat, a_mat, d_tiled)
    return yg.reshape(g, bsz, nchunks, t_len, S5_GROUP).transpose(1, 2, 3, 0, 4).reshape(bsz, seq_len, w)


def _outproj_kernel(h_ref, gt_ref, yr_ref, ys_ref, yg_ref, wglu_ref, bglu_ref, wout_ref, o_ref):
    y = ys_ref[...]
    z = y * (0.5 * (1.0 + jnp.tanh(math.sqrt(2.0 / math.pi) * (y + 0.044715 * (y * y * y)))))
    gl = jnp.dot(z.astype(BF16), wglu_ref[...], preferred_element_type=F32) + bglu_ref[...]
    s5 = (z * _sigmoid(gl)).astype(BF16)
    ycat = jnp.concatenate([yr_ref[...], s5, yg_ref[...]], axis=1)
    o_ref[...] = h_ref[...] + gt_ref[0] * jnp.dot(ycat, wout_ref[...], preferred_element_type=F32)


def _outproj(h, seq_len, gate, y_ret, y_s5, y_gla, w_glu, b_glu, w_out, *, tm=512):
    m, d = h.shape
    tm = min(tm, seq_len)
    per_batch = seq_len // tm
    w_s5 = y_s5.shape[1]
    row = lambda width: pl.BlockSpec((tm, width), lambda i: (i, 0))
    whole = lambda a: pl.BlockSpec(a.shape, lambda i: (0, 0), pipeline_mode=pl.Buffered(1))
    wglu = w_glu.astype(BF16)
    bglu = b_glu.astype(F32).reshape(1, w_s5)
    wout = w_out.astype(BF16)
    return pl.pallas_call(
        _outproj_kernel,
        grid=(m // tm,),
        in_specs=[
            row(d),
            pl.BlockSpec((1, 1, d), lambda i: (i // per_batch, 0, 0)),
            row(y_ret.shape[1]), row(w_s5), row(y_gla.shape[1]),
            whole(wglu), whole(bglu), whole(wout),
        ],
        out_specs=row(d),
        out_shape=jax.ShapeDtypeStruct((m, d), F32),
        compiler_params=_params("parallel"),
        name="outproj",
    )(h, gate, y_ret, y_s5, y_gla, wglu, bglu, wout)


def kernel(x, c, w_ada, b_ada, g_ffn1, ffn1_w1, ffn1_w3, ffn1_w2, g_mix, w_in, s5_lam_re, s5_lam_im, s5_log_dt, s5_b_re, s5_b_im, s5_c_re, s5_c_im, s5_d, s5_w_glu, s5_b_glu, gla_w_gate, gla_b_gate, w_out, g_ffn2, ffn2_w1, ffn2_w3, ffn2_w2, g_final):
    bsz, seq_len, d = x.shape
    depth = w_ada.shape[0]
    mod = _ada_mod(c, w_ada, b_ada).reshape(depth, bsz, N_MOD, 1, d)
    tables = _ret_tables(seq_len)
    ret_w = RET_HEADS * (2 * RET_DK + 2 * RET_DV)
    s5_w = s5_d.shape[1]
    h = x.reshape(bsz * seq_len, d)
    for l in range(depth):
        sh1, sc1, gt1, sh2, sc2, gt2, sh3, sc3, gt3 = (mod[l, :, i] for i in range(N_MOD))
        h = _ffn(h, seq_len, g_ffn1[l], sh1, sc1, gt1, ffn1_w1[l], ffn1_w3[l], ffn1_w2[l])
        proj = _inproj(h, seq_len, g_mix[l], sh2, sc2, w_in[l]).reshape(bsz, seq_len, -1)
        y_ret = _retention(proj, tables)
        y_s5 = _s5(proj[:, :, ret_w:ret_w + s5_w], s5_lam_re[l], s5_lam_im[l], s5_log_dt[l],
                   s5_b_re[l], s5_b_im[l], s5_c_re[l], s5_c_im[l], s5_d[l])
        y_gla = _gla(proj, gla_w_gate[l], gla_b_gate[l], ret_w + s5_w)
        h = _outproj(h, seq_len, gt2, y_ret.reshape(bsz * seq_len, -1), y_s5.reshape(bsz * seq_len, -1),
                     y_gla.reshape(bsz * seq_len, -1), s5_w_glu[l], s5_b_glu[l], w_out[l])
        h = _ffn(h, seq_len, g_ffn2[l], sh3, sc3, gt3, ffn2_w1[l], ffn2_w3[l], ffn2_w2[l],
                 g_final if l == depth - 1 else None)
    return h.reshape(bsz, seq_len, d)
```

```python
import functools
import math

import jax
import jax.numpy as jnp
from jax import lax
from jax.experimental import pallas as pl
from jax.experimental.pallas import tpu as pltpu

F32 = jnp.float32
BF16 = jnp.bfloat16
HIGHEST = lax.Precision.HIGHEST

RET_HEADS = 4
RET_DK = 128
RET_DV = 256
RET_CHUNK = 256
ROPE_BASE = 10000.0
S5_GROUP = 16
S5_STATE = 64
S5_CHUNK = 32
GLA_HEADS = 4
GLA_DK = 64
GLA_DV = 128
GLA_RANK = 16
GLA_TAU = 16.0
GLA_CHUNK = 64
GLA_BLOCK = 256
FFN_RES = 0.5
N_MOD = 9
EPS = 1e-6

LANES = 128
MXU_N = 256
VMEM_LIMIT = 56 << 20

NT_DIMS = (((1,), (1,)), ((), ()))
TN_DIMS = (((0,), (0,)), ((), ()))


def _params(*semantics):
    return pltpu.CompilerParams(dimension_semantics=semantics, vmem_limit_bytes=VMEM_LIMIT)


def _sigmoid(x):
    return 1.0 / (1.0 + jnp.exp(-x))


def _norm_mod(x, g, shift, scale):
    ms = jnp.mean(x * x, axis=-1, keepdims=True)
    return (x * lax.rsqrt(ms + EPS) * g) * (1.0 + scale) + shift


def _ada_kernel(c_ref, w_ref, b_ref, o_ref):
    c = c_ref[...]
    cond = c * _sigmoid(c)
    o_ref[0] = jnp.dot(cond, w_ref[0], preferred_element_type=F32, precision=HIGHEST) + b_ref[0]


def _ada_mod(c, w_ada, b_ada):
    depth, d, n = w_ada.shape
    bsz = c.shape[0]
    rows = -(-bsz // 8) * 8
    c_pad = jnp.zeros((rows, d), F32).at[:bsz].set(c)
    tn = n // 16
    out = pl.pallas_call(
        _ada_kernel,
        grid=(depth, n // tn),
        in_specs=[
            pl.BlockSpec((rows, d), lambda l, j: (0, 0)),
            pl.BlockSpec((1, d, tn), lambda l, j: (l, 0, j)),
            pl.BlockSpec((1, 1, tn), lambda l, j: (l, 0, j)),
        ],
        out_specs=pl.BlockSpec((1, rows, tn), lambda l, j: (l, 0, j)),
        out_shape=jax.ShapeDtypeStruct((depth, rows, n), F32),
        compiler_params=_params("parallel", "parallel"),
        name="ada_mod",
    )(c_pad, w_ada, b_ada.reshape(depth, 1, n))
    return out[:, :bsz]


def _ffn_kernel(h_ref, g_ref, sh_ref, sc_ref, gt_ref, w1_ref, w3_ref, w2_ref, *rest, tf, nj, final):
    if final:
        gf_ref, o_ref, u_scr, acc_scr = rest
    else:
        o_ref, u_scr, acc_scr = rest
    j = pl.program_id(1)

    @pl.when(j == 0)
    def _():
        u_scr[...] = _norm_mod(h_ref[...], g_ref[...], sh_ref[0], sc_ref[0]).astype(BF16)
        acc_scr[...] = jnp.zeros_like(acc_scr)

    u = u_scr[...]
    acts = []
    for c0 in range(0, tf, MXU_N):
        a = jnp.dot(u, w1_ref[:, c0:c0 + MXU_N], preferred_element_type=F32)
        b = jnp.dot(u, w3_ref[:, c0:c0 + MXU_N], preferred_element_type=F32)
        acts.append((a * _sigmoid(a) * b).astype(BF16))
    act = jnp.concatenate(acts, axis=1)
    acc_scr[...] += jnp.dot(act, w2_ref[...], preferred_element_type=F32)

    @pl.when(j == nj - 1)
    def _():
        out = h_ref[...] + (FFN_RES * gt_ref[0]) * acc_scr[...]
        if final:
            ms = jnp.mean(out * out, axis=-1, keepdims=True)
            out = out * lax.rsqrt(ms + EPS) * gf_ref[...]
        o_ref[...] = out


def _ffn(h, seq_len, g, shift, scale, gate, w1, w3, w2, layer, g_final=None, *, tm=512, tf=512):
    m, d = h.shape
    dff = w1.shape[2]
    tm = min(tm, seq_len)
    nj = dff // tf
    per_batch = seq_len // tm
    final = g_final is not None
    vec = lambda i, j: (i // per_batch, 0, 0)
    in_specs = [
        pl.BlockSpec((tm, d), lambda i, j: (i, 0)),
        pl.BlockSpec((1, d), lambda i, j: (0, 0)),
        pl.BlockSpec((1, 1, d), vec),
        pl.BlockSpec((1, 1, d), vec),
        pl.BlockSpec((1, 1, d), vec),
        pl.BlockSpec((None, d, tf), lambda i, j: (layer, 0, j)),
        pl.BlockSpec((None, d, tf), lambda i, j: (layer, 0, j)),
        pl.BlockSpec((None, tf, d), lambda i, j: (layer, j, 0)),
    ]
    args = [h, g.reshape(1, d), shift, scale, gate, w1, w3, w2]
    if final:
        in_specs.append(pl.BlockSpec((1, d), lambda i, j: (0, 0)))
        args.append(g_final.reshape(1, d))
    return pl.pallas_call(
        functools.partial(_ffn_kernel, tf=tf, nj=nj, final=final),
        grid=(m // tm, nj),
        in_specs=in_specs,
        out_specs=pl.BlockSpec((tm, d), lambda i, j: (i, 0)),
        out_shape=jax.ShapeDtypeStruct((m, d), F32),
        scratch_shapes=[pltpu.VMEM((tm, d), BF16), pltpu.VMEM((tm, d), F32)],
        compiler_params=_params("parallel", "arbitrary"),
        name="ffn",
    )(*args)


def _inproj_kernel(h_ref, g_ref, sh_ref, sc_ref, w_ref, o_ref):
    u = _norm_mod(h_ref[...], g_ref[...], sh_ref[0], sc_ref[0]).astype(BF16)
    o_ref[...] = jnp.dot(u, w_ref[...], preferred_element_type=F32)


def _inproj(h, seq_len, g, shift, scale, w, layer, *, tm=256):
    m, d = h.shape
    d_pad = w.shape[2]
    tm = min(tm, seq_len)
    per_batch = seq_len // tm
    vec = lambda i: (i // per_batch, 0, 0)
    return pl.pallas_call(
        _inproj_kernel,
        grid=(m // tm,),
        in_specs=[
            pl.BlockSpec((tm, d), lambda i: (i, 0)),
            pl.BlockSpec((1, d), lambda i: (0, 0)),
            pl.BlockSpec((1, 1, d), vec),
            pl.BlockSpec((1, 1, d), vec),
            pl.BlockSpec((None, d, d_pad), lambda i: (layer, 0, 0), pipeline_mode=pl.Buffered(1)),
        ],
        out_specs=pl.BlockSpec((tm, d_pad), lambda i: (i, 0)),
        out_shape=jax.ShapeDtypeStruct((m, d_pad), F32),
        compiler_params=_params("parallel"),
        name="inproj",
    )(h, g.reshape(1, d), shift, scale, w)


def _ret_tables(seq_len):
    c = RET_CHUNK
    pos = jnp.arange(seq_len, dtype=F32)
    inv_freq = ROPE_BASE ** (-jnp.arange(0, RET_DK, 2, dtype=F32) / RET_DK)
    ang = pos[:, None] * inv_freq[None, :]
    cos, sin = jnp.cos(ang), jnp.sin(ang)
    cs = jnp.concatenate([cos, cos], axis=-1)
    sn = jnp.concatenate([-sin, sin], axis=-1)
    lg = jnp.log1p(-jnp.exp2(-5.0 - jnp.arange(RET_HEADS, dtype=F32)))
    idx = jnp.arange(c, dtype=F32)
    dist = jnp.abs(idx[:, None] - idx[None, :])
    dmat = jnp.exp(dist[None] * lg[:, None, None])
    rows = jnp.stack([c - 1.0 - idx, idx, idx + 1.0, c - idx])
    dvec = jnp.exp(rows[None] * lg[:, None, None])
    dvec = jnp.broadcast_to(dvec[..., None], dvec.shape + (RET_DK,))
    gc = jnp.broadcast_to(jnp.exp(c * lg)[:, None, None], (RET_HEADS, 1, RET_DV))
    return cs, sn, dmat, dvec, gc


def _ret_kernel(q_ref, k_ref, v_ref, g_ref, cs_ref, sn_ref, dmat_ref, dvec_ref, gc_ref, o_ref,
                q_scr, k_scr, qf_scr, qb_scr, kf_scr, kb_scr, of_scr, ob_scr, sf_scr, sb_scr,
                *, nchunks):
    c = RET_CHUNK
    scale = RET_DK ** -0.5
    chunk_rows = lambda n: pl.ds(pl.multiple_of(n * c, c), c)

    def prepare(n, carry):
        rows = chunk_rows(n)
        cs = cs_ref[rows, :]
        sn = sn_ref[rows, :]
        q = q_ref[0, rows, :]
        k = k_ref[0, rows, :]
        q = (q * cs + pltpu.roll(q, RET_DK // 2, 1) * sn) * scale
        k = k * cs + pltpu.roll(k, RET_DK // 2, 1) * sn
        q_scr[rows, :] = q.astype(BF16)
        k_scr[rows, :] = k.astype(BF16)
        kf_scr[rows, :] = (k * dvec_ref[0, 0]).astype(BF16)
        kb_scr[rows, :] = (k * dvec_ref[0, 1]).astype(BF16)
        qf_scr[rows, :] = (q * dvec_ref[0, 2]).astype(BF16)
        qb_scr[rows, :] = (q * dvec_ref[0, 3]).astype(BF16)
        return carry

    lax.fori_loop(0, nchunks, prepare, 0, unroll=4)
    sf_scr[...] = jnp.zeros_like(sf_scr)
    sb_scr[...] = jnp.zeros_like(sb_scr)
    gc = gc_ref[0]

    def recur(n, carry):
        rf = chunk_rows(n)
        vf = v_ref[0, rf, :].astype(BF16)
        scores = lax.dot_general(q_scr[rf, :], k_scr[rf, :], NT_DIMS,
                                 preferred_element_type=F32) * dmat_ref[0]
        s = sf_scr[...]
        of_scr[rf, :] = (jnp.dot(scores.astype(BF16), vf, preferred_element_type=F32)
                         + jnp.dot(qf_scr[rf, :], s.astype(BF16), preferred_element_type=F32))
        sf_scr[...] = gc * s + lax.dot_general(kf_scr[rf, :], vf, TN_DIMS,
                                               preferred_element_type=F32)
        rb = chunk_rows(nchunks - 1 - n)
        vb = v_ref[0, rb, :].astype(BF16)
        s = sb_scr[...]
        ob_scr[rb, :] = jnp.dot(qb_scr[rb, :], s.astype(BF16), preferred_element_type=F32)
        sb_scr[...] = gc * s + lax.dot_general(kb_scr[rb, :], vb, TN_DIMS,
                                               preferred_element_type=F32)
        return carry

    lax.fori_loop(0, nchunks, recur, 0, unroll=4)

    def finish(n, carry):
        rows = chunk_rows(n)
        o = of_scr[rows, :] + ob_scr[rows, :]
        oc = o - jnp.mean(o, axis=-1, keepdims=True)
        ln = oc * lax.rsqrt(jnp.mean(oc * oc, axis=-1, keepdims=True) + EPS)
        g = g_ref[0, rows, :]
        o_ref[0, rows, :] = (g * _sigmoid(g) * ln).astype(o_ref.dtype)
        return carry

    lax.fori_loop(0, nchunks, finish, 0, unroll=4)


def _retention(proj, tables):
    bsz, seq_len, _ = proj.shape
    cs, sn, dmat, dvec, gc = tables
    h = RET_HEADS
    kq, kk = 0, h
    kv, kg = (2 * h * RET_DK) // RET_DV, (2 * h * RET_DK + h * RET_DV) // RET_DV
    tok = lambda off: (lambda b, i: (b, 0, off + i))
    return pl.pallas_call(
        functools.partial(_ret_kernel, nchunks=seq_len // RET_CHUNK),
        grid=(bsz, h),
        in_specs=[
            pl.BlockSpec((1, seq_len, RET_DK), tok(kq)),
            pl.BlockSpec((1, seq_len, RET_DK), tok(kk)),
            pl.BlockSpec((1, seq_len, RET_DV), tok(kv)),
            pl.BlockSpec((1, seq_len, RET_DV), tok(kg)),
            pl.BlockSpec((seq_len, RET_DK), lambda b, i: (0, 0), pipeline_mode=pl.Buffered(1)),
            pl.BlockSpec((seq_len, RET_DK), lambda b, i: (0, 0), pipeline_mode=pl.Buffered(1)),
            pl.BlockSpec((1, RET_CHUNK, RET_CHUNK), lambda b, i: (i, 0, 0)),
            pl.BlockSpec((1, 4, RET_CHUNK, RET_DK), lambda b, i: (i, 0, 0, 0)),
            pl.BlockSpec((1, 1, RET_DV), lambda b, i: (i, 0, 0)),
        ],
        out_specs=pl.BlockSpec((1, seq_len, RET_DV), lambda b, i: (b, 0, i)),
        out_shape=jax.ShapeDtypeStruct((bsz, seq_len, h * RET_DV), BF16),
        scratch_shapes=(
            [pltpu.VMEM((seq_len, RET_DK), BF16)] * 6
            + [pltpu.VMEM((seq_len, RET_DV), F32)] * 2
            + [pltpu.VMEM((RET_DK, RET_DV), F32)] * 2
        ),
        compiler_params=_params("parallel", "parallel"),
        name="retention",
    )(proj, proj, proj, proj, cs, sn, dmat, dvec, gc)


def _gla_kernel(q_ref, k_ref, v_ref, g_ref, lr_ref, wf_ref, wb_ref, bf_ref, bb_ref, o_ref,
                q0_scr, q1_scr, kin_scr, kst_scr, dec_scr, o_scr, st_scr, *, nblocks):
    c, r = GLA_CHUNK, GLA_BLOCK
    per = r // c
    dk2, dv2 = 2 * GLA_DK, 2 * GLA_DV
    scale = GLA_DK ** -0.5
    block_rows = lambda m: pl.ds(pl.multiple_of(m * r, r), r)

    ri = lax.broadcasted_iota(jnp.int32, (r, r), 0)
    ci = lax.broadcasted_iota(jnp.int32, (r, r), 1)
    same_chunk = (ri // c) == (ci // c)
    lane = lax.broadcasted_iota(jnp.int32, (r, dk2), 1)
    se = lax.broadcasted_iota(jnp.int32, (dv2, dk2), 0)
    sd = lax.broadcasted_iota(jnp.int32, (dv2, dk2), 1)
    same_head = jnp.where(se < GLA_DV, jnp.where(sd < GLA_DK, 1.0, 0.0),
                          jnp.where(sd >= GLA_DK, 1.0, 0.0)).astype(F32)

    def direction(w_ref, b_ref, reverse):
        if reverse:
            cum = same_chunk & (ci >= ri)
            keep = same_chunk & (ci > ri)
        else:
            cum = same_chunk & (ci <= ri)
            keep = same_chunk & (ci <= ri)
        sum_mat = jnp.concatenate([jnp.where(cum, 1.0, 0.0), jnp.where(same_chunk, 1.0, 0.0)],
                                  axis=0).astype(BF16)

        def prepare(m, carry):
            rows = block_rows(m)
            x = jnp.dot(lr_ref[0, rows, :], w_ref[0], preferred_element_type=F32,
                        precision=HIGHEST) + b_ref[0]
            la = (jnp.minimum(x, 0.0) - jnp.log(1.0 + jnp.exp(-jnp.abs(x)))) * (1.0 / GLA_TAU)
            hi = la.astype(BF16)
            rest = la - hi.astype(F32)
            mid = rest.astype(BF16)
            lo = (rest - mid.astype(F32)).astype(BF16)
            sums = jnp.dot(sum_mat, jnp.concatenate([hi, mid, lo], axis=1),
                           preferred_element_type=F32)
            sums = sums[:, :dk2] + sums[:, dk2:2 * dk2] + sums[:, 2 * dk2:]
            b, total = sums[:r], sums[r:]
            q_in = q_ref[0, rows, :] * scale * jnp.exp(b)
            k = k_ref[0, rows, :]
            q0_scr[rows, :] = jnp.where(lane < GLA_DK, q_in, 0.0).astype(BF16)
            q1_scr[rows, :] = jnp.where(lane >= GLA_DK, q_in, 0.0).astype(BF16)
            kin_scr[rows, :] = (k * jnp.exp(-b)).astype(BF16)
            kst_scr[rows, :] = (k * jnp.exp(total - b)).astype(BF16)
            dec_scr[rows, :] = jnp.exp(total)
            return carry

        lax.fori_loop(0, nblocks, prepare, 0, unroll=4)
        st_scr[...] = jnp.zeros_like(st_scr)

        def recur(t, carry):
            rows = block_rows(nblocks - 1 - t if reverse else t)
            q0 = q0_scr[rows, :]
            q1 = q1_scr[rows, :]
            kin = kin_scr[rows, :]
            kst = kst_scr[rows, :]
            dec = dec_scr[rows, :]
            vb = v_ref[0, rows, :].astype(BF16)
            intra = []
            for hh, qh in enumerate((q0, q1)):
                scores = lax.dot_general(qh, kin, NT_DIMS, preferred_element_type=F32)
                scores = jnp.where(keep, scores, 0.0).astype(BF16)
                intra.append(jnp.dot(scores, vb[:, hh * GLA_DV:(hh + 1) * GLA_DV],
                                     preferred_element_type=F32))
            qin = q0 + q1
            st = st_scr[...]
            inter = [None] * per
            for i in (range(per - 1, -1, -1) if reverse else range(per)):
                sl = slice(i * c, (i + 1) * c)
                inter[i] = lax.dot_general(qin[sl], st.astype(BF16), NT_DIMS,
                                           preferred_element_type=F32)
                kv_t = lax.dot_general(vb[sl], kst[sl], TN_DIMS, preferred_element_type=F32)
                st = st * dec[i * c:i * c + 1, :] + kv_t * same_head
            st_scr[...] = st
            o = jnp.concatenate(intra, axis=1) + jnp.concatenate(inter, axis=0)
            if not reverse:
                o_scr[rows, :] = o
            else:
                o = o + o_scr[rows, :]
                normed = []
                for hh in range(2):
                    oh = o[:, hh * GLA_DV:(hh + 1) * GLA_DV]
                    normed.append(oh * lax.rsqrt(jnp.mean(oh * oh, axis=-1, keepdims=True) + EPS))
                g = g_ref[0, rows, :]
                o_ref[0, rows, :] = (g * _sigmoid(g)
                                     * jnp.concatenate(normed, axis=1)).astype(o_ref.dtype)
            return carry

        lax.fori_loop(0, nblocks, recur, 0, unroll=4)

    direction(wf_ref, bf_ref, False)
    direction(wb_ref, bb_ref, True)


def _gla(proj, w_gate, b_gate, col0):
    bsz, seq_len, _ = proj.shape
    pairs = GLA_HEADS // 2
    dk2, dv2 = 2 * GLA_DK, 2 * GLA_DV
    c_q = col0
    c_k = c_q + GLA_HEADS * GLA_DK
    c_v = c_k + GLA_HEADS * GLA_DK
    c_g = c_v + GLA_HEADS * GLA_DV
    c_lr = c_g + GLA_HEADS * GLA_DV
    wg = w_gate.astype(F32).reshape(2, GLA_RANK, pairs, dk2).transpose(0, 2, 1, 3)
    wf = jnp.zeros((pairs, LANES, dk2), F32).at[:, :GLA_RANK].set(wg[0])
    wb = jnp.zeros((pairs, LANES, dk2), F32).at[:, GLA_RANK:2 * GLA_RANK].set(wg[1])
    bg = b_gate.astype(F32).reshape(2, pairs, 1, dk2)
    tok = lambda col, width: (lambda b, p: (b, 0, col // width + p))
    mat = lambda b, p: (p, 0, 0)
    return pl.pallas_call(
        functools.partial(_gla_kernel, nblocks=seq_len // GLA_BLOCK),
        grid=(bsz, pairs),
        in_specs=[
            pl.BlockSpec((1, seq_len, dk2), tok(c_q, dk2)),
            pl.BlockSpec((1, seq_len, dk2), tok(c_k, dk2)),
            pl.BlockSpec((1, seq_len, dv2), tok(c_v, dv2)),
            pl.BlockSpec((1, seq_len, dv2), tok(c_g, dv2)),
            pl.BlockSpec((1, seq_len, LANES), lambda b, p: (b, 0, c_lr // LANES)),
            pl.BlockSpec((1, LANES, dk2), mat),
            pl.BlockSpec((1, LANES, dk2), mat),
            pl.BlockSpec((1, 1, dk2), mat),
            pl.BlockSpec((1, 1, dk2), mat),
        ],
        out_specs=pl.BlockSpec((1, seq_len, dv2), lambda b, p: (b, 0, p)),
        out_shape=jax.ShapeDtypeStruct((bsz, seq_len, GLA_HEADS * GLA_DV), BF16),
        scratch_shapes=(
            [pltpu.VMEM((seq_len, dk2), BF16)] * 4
            + [pltpu.VMEM((seq_len, dk2), F32), pltpu.VMEM((seq_len, dv2), F32),
               pltpu.VMEM((dv2, dk2), F32)]
        ),
        compiler_params=_params("parallel", "parallel"),
        name="gla",
    )(proj, proj, proj, proj, proj, wf, wb, bg[0], bg[1])


def _swap(x):
    return pltpu.roll(x, S5_STATE, 1)


def _s5_prep_kernel(lre_ref, lim_ref, ldt_ref, brr_ref, bis_ref, cr_ref, ci_ref,
                    m_ref, e_ref, ft_ref, a_ref, *, nsteps):
    t_len = S5_CHUNK
    gsz = S5_GROUP
    width = gsz * t_len
    tp = t_len + 8
    first = lax.broadcasted_iota(jnp.int32, (1, LANES), 1) < S5_STATE
    one_zero = jnp.where(first, 1.0, 0.0).astype(F32)
    sign = jnp.where(first, 1.0, -1.0).astype(F32)

    def packed_exp(t, are, aim):
        ph = t * aim
        return jnp.exp(t * are) * jnp.where(first, jnp.cos(ph), jnp.sin(ph))

    def expand(table, tsel, nrows):
        r = lax.broadcasted_iota(jnp.int32, (nrows, tp), 0)
        tcol = lax.broadcasted_iota(jnp.int32, (nrows, tp), 1)
        sel = jnp.where(tsel(r // gsz) == tcol, 1.0, 0.0).astype(F32)
        return jnp.dot(sel, table, preferred_element_type=F32, precision=HIGHEST)

    def dup_re(x):
        return jnp.where(first, x, _swap(x))

    def dup_im_signed(x):
        return jnp.where(first, -_swap(x), x)

    panels = []
    for d in range(2):
        lre = lre_ref[0, d:d + 1, :]
        lim = lim_ref[0, d:d + 1, :]
        dt = jnp.exp(ldt_ref[0, d:d + 1, :])
        are, aim = lre * dt, lim * dt
        tcol = lax.broadcasted_iota(jnp.int32, (tp, 1), 0).astype(F32)
        pw = packed_exp(tcol, are, aim)
        num = pw[1:2, :] - one_zero
        inv_den = 1.0 / (lre * lre + lim * lim)
        coef = num * (lre * inv_den) + _swap(num) * (lim * inv_den * sign)
        bbar = coef * brr_ref[0, d] + _swap(coef) * bis_ref[0, d]
        tsel_e = (lambda j: t_len - 1 - j) if d == 0 else (lambda j: j)
        pe = expand(pw, tsel_e, width)
        e_mat = (pe * jnp.tile(dup_re(bbar), (t_len, 1))
                 + _swap(pe) * jnp.tile(dup_im_signed(bbar), (t_len, 1)))
        e_ref[0, :, d * LANES:(d + 1) * LANES] = e_mat.astype(BF16)
        tsel_g = (lambda s: s) if d == 0 else (lambda s: t_len - s)
        pg = expand(pw, tsel_g, width + gsz)
        g_mat = (pg * jnp.tile(cr_ref[0, d], (t_len + 1, 1))
                 + _swap(pg) * jnp.tile(ci_ref[0, d], (t_len + 1, 1)))
        if d == 0:
            ft_ref[0, d] = g_mat[gsz:, :].astype(BF16)
            g_panel = g_mat[:width, :]
        else:
            ft_ref[0, d] = g_mat[:width, :].astype(BF16)
            g_panel = g_mat[gsz:, :]
        panels.append(lax.dot_general(bbar, g_panel, NT_DIMS, preferred_element_type=F32,
                                      precision=HIGHEST))
        kcol = lax.broadcasted_iota(jnp.int32, (8, 1), 0)
        tk = (t_len * jnp.left_shift(1, kcol)).astype(F32)
        ak = packed_exp(tk, are, aim)
        a_ref[0, d, 0:8, :] = dup_re(ak)
        a_ref[0, d, 8:16, :] = dup_im_signed(ak)

    p_fwd, p_bwd = panels
    lane = lax.broadcasted_iota(jnp.int32, (gsz, width), 1)
    for j in range(t_len):
        blk = jnp.where(lane >= gsz * j, pltpu.roll(p_fwd, gsz * j, 1) if j else p_fwd, 0.0)
        shift = (gsz * (j + 1)) % width
        blk = blk + jnp.where(lane < gsz * (j + 1), pltpu.roll(p_bwd, shift, 1) if shift else p_bwd, 0.0)
        m_ref[0, gsz * j:gsz * (j + 1), :] = blk.astype(BF16)


def _s5_apply_kernel(u_ref, m_ref, e_ref, ft_ref, a_ref, d_ref, y_ref, *, nchunks, nsteps):
    u = u_ref[0]
    ub = u.astype(BF16)
    rows = u.shape[0]
    y = jnp.dot(ub, m_ref[0], preferred_element_type=F32) + u * d_ref[0]
    s = jnp.dot(ub, e_ref[0], preferred_element_type=F32)
    nidx = lax.broadcasted_iota(jnp.int32, (rows, LANES), 0) % nchunks
    for d in range(2):
        x = s[:, d * LANES:(d + 1) * LANES]

        def shifted(val, sh):
            if d == 0:
                return jnp.where(nidx >= sh, pltpu.roll(val, sh, 0), 0.0)
            return jnp.where(nidx < nchunks - sh, pltpu.roll(val, rows - sh, 0), 0.0)

        for k in range(nsteps):
            xs = shifted(x, 1 << k)
            x = x + a_ref[0, d, k:k + 1, :] * xs + a_ref[0, d, 8 + k:9 + k, :] * _swap(xs)
        xp = shifted(x, 1).astype(BF16)
        y = y + lax.dot_general(xp, ft_ref[0, d], NT_DIMS, preferred_element_type=F32)
    y_ref[0] = y


def _s5(su, lam_re, lam_im, log_dt, b_re, b_im, c_re, c_im, d_skip):
    bsz, seq_len, w = su.shape
    g = w // S5_GROUP
    t_len = S5_CHUNK
    nchunks = seq_len // t_len
    nsteps = max(1, (nchunks - 1).bit_length())
    rows = bsz * nchunks
    width = S5_GROUP * t_len
    dup = lambda a: jnp.concatenate([a, a], axis=-1)
    gd = lambda a: jnp.moveaxis(a.astype(F32), 0, 1)
    lre = dup(gd(lam_re))
    lim = dup(gd(lam_im))
    ldt = jnp.broadcast_to(gd(log_dt)[..., None], (g, 2, LANES))
    bt_re = jnp.swapaxes(gd(b_re), -1, -2)
    bt_im = jnp.swapaxes(gd(b_im), -1, -2)
    brr = jnp.concatenate([bt_re, bt_re], axis=-1)
    bis = jnp.concatenate([-bt_im, bt_im], axis=-1)
    cre, cim = gd(c_re), gd(c_im)
    cr = jnp.concatenate([cre, -cre], axis=-1)
    ci = jnp.concatenate([-cim, -cim], axis=-1)
    vec3 = pl.BlockSpec((1, 2, LANES), lambda i: (i, 0, 0))
    mat4 = pl.BlockSpec((1, 2, S5_GROUP, LANES), lambda i: (i, 0, 0, 0))
    m_mat, e_mat, ft_mat, a_mat = pl.pallas_call(
        functools.partial(_s5_prep_kernel, nsteps=nsteps),
        grid=(g,),
        in_specs=[vec3, vec3, vec3, mat4, mat4, mat4, mat4],
        out_specs=[
            pl.BlockSpec((1, width, width), lambda i: (i, 0, 0)),
            pl.BlockSpec((1, width, 2 * LANES), lambda i: (i, 0, 0)),
            pl.BlockSpec((1, 2, width, LANES), lambda i: (i, 0, 0, 0)),
            pl.BlockSpec((1, 2, 16, LANES), lambda i: (i, 0, 0, 0)),
        ],
        out_shape=[
            jax.ShapeDtypeStruct((g, width, width), BF16),
            jax.ShapeDtypeStruct((g, width, 2 * LANES), BF16),
            jax.ShapeDtypeStruct((g, 2, width, LANES), BF16),
            jax.ShapeDtypeStruct((g, 2, 16, LANES), F32),
        ],
        compiler_params=_params("parallel"),
        name="s5_prep",
    )(lre, lim, ldt, brr, bis, cr, ci)
    ug = su.reshape(bsz, nchunks, t_len, g, S5_GROUP).transpose(3, 0, 1, 2, 4).reshape(g, rows, width)
    d_tiled = jnp.tile(d_skip.astype(F32).reshape(g, 1, S5_GROUP), (1, 1, t_len))
    yg = pl.pallas_call(
        functools.partial(_s5_apply_kernel, nchunks=nchunks, nsteps=nsteps),
        grid=(g,),
        in_specs=[
            pl.BlockSpec((1, rows, width), lambda i: (i, 0, 0)),
            pl.BlockSpec((1, width, width), lambda i: (i, 0, 0)),
            pl.BlockSpec((1, width, 2 * LANES), lambda i: (i, 0, 0)),
            pl.BlockSpec((1, 2, width, LANES), lambda i: (i, 0, 0, 0)),
            pl.BlockSpec((1, 2, 16, LANES), lambda i: (i, 0, 0, 0)),
            pl.BlockSpec((1, 1, width), lambda i: (i, 0, 0)),
        ],
        out_specs=pl.BlockSpec((1, rows, width), lambda i: (i, 0, 0)),
        out_shape=jax.ShapeDtypeStruct((g, rows, width), F32),
        compiler_params=_params("parallel"),
        name="s5_apply",
    )(ug, m_mat, e_mat, ft_mat, a_mat, d_tiled)
    return yg.reshape(g, bsz, nchunks, t_len, S5_GROUP).transpose(1, 2, 3, 0, 4).reshape(bsz, seq_len, w)


def _outproj_kernel(h_ref, gt_ref, yr_ref, ys_ref, yg_ref, wglu_ref, bglu_ref, wout_ref, o_ref):
    y = ys_ref[...]
    z = y * (0.5 * (1.0 + jnp.tanh(math.sqrt(2.0 / math.pi) * (y + 0.044715 * (y * y * y)))))
    gl = jnp.dot(z.astype(BF16), wglu_ref[...], preferred_element_type=F32) + bglu_ref[...]
    s5 = (z * _sigmoid(gl)).astype(BF16)
    ycat = jnp.concatenate([yr_ref[...], s5, yg_ref[...]], axis=1)
    o_ref[...] = h_ref[...] + gt_ref[0] * jnp.dot(ycat, wout_ref[...], preferred_element_type=F32)


def _outproj(h, seq_len, gate, y_ret, y_s5, y_gla, wglu, b_glu, wout, layer, *, tm=512):
    m, d = h.shape
    tm = min(tm, seq_len)
    per_batch = seq_len // tm
    w_s5 = y_s5.shape[1]
    row = lambda width: pl.BlockSpec((tm, width), lambda i: (i, 0))
    whole = lambda a: pl.BlockSpec((None,) + a.shape[1:], lambda i: (layer, 0, 0),
                                   pipeline_mode=pl.Buffered(1))
    bglu = b_glu.astype(F32).reshape(-1, 1, w_s5)
    return pl.pallas_call(
        _outproj_kernel,
        grid=(m // tm,),
        in_specs=[
            row(d),
            pl.BlockSpec((1, 1, d), lambda i: (i // per_batch, 0, 0)),
            row(y_ret.shape[1]), row(w_s5), row(y_gla.shape[1]),
            whole(wglu), whole(bglu), whole(wout),
        ],
        out_specs=row(d),
        out_shape=jax.ShapeDtypeStruct((m, d), F32),
        compiler_params=_params("parallel"),
        name="outproj",
    )(h, gate, y_ret, y_s5, y_gla, wglu, bglu, wout)


def kernel(x, c, w_ada, b_ada, g_ffn1, ffn1_w1, ffn1_w3, ffn1_w2, g_mix, w_in, s5_lam_re, s5_lam_im, s5_log_dt, s5_b_re, s5_b_im, s5_c_re, s5_c_im, s5_d, s5_w_glu, s5_b_glu, gla_w_gate, gla_b_gate, w_out, g_ffn2, ffn2_w1, ffn2_w3, ffn2_w2, g_final):
    bsz, seq_len, d = x.shape
    depth = w_ada.shape[0]
    mod = _ada_mod(c, w_ada, b_ada).reshape(depth, bsz, N_MOD, 1, d)
    tables = _ret_tables(seq_len)
    ret_w = RET_HEADS * (2 * RET_DK + 2 * RET_DV)
    s5_w = s5_d.shape[1]
    h = x.reshape(bsz * seq_len, d)
    ffn1_w1, ffn1_w3, ffn1_w2, ffn2_w1, ffn2_w3, ffn2_w2 = (
        w.astype(BF16) for w in (ffn1_w1, ffn1_w3, ffn1_w2, ffn2_w1, ffn2_w3, ffn2_w2))
    d_in = w_in.shape[2]
    w_in = jnp.pad(w_in.astype(BF16), ((0, 0), (0, 0), (0, -d_in % LANES)))
    w_glu, w_out = s5_w_glu.astype(BF16), w_out.astype(BF16)
    for l in range(depth):
        sh1, sc1, gt1, sh2, sc2, gt2, sh3, sc3, gt3 = (mod[l, :, i] for i in range(N_MOD))
        h = _ffn(h, seq_len, g_ffn1[l], sh1, sc1, gt1, ffn1_w1, ffn1_w3, ffn1_w2, l)
        proj = _inproj(h, seq_len, g_mix[l], sh2, sc2, w_in, l).reshape(bsz, seq_len, -1)
        y_ret = _retention(proj, tables)
        y_s5 = _s5(proj[:, :, ret_w:ret_w + s5_w], s5_lam_re[l], s5_lam_im[l], s5_log_dt[l],
                   s5_b_re[l], s5_b_im[l], s5_c_re[l], s5_c_im[l], s5_d[l])
        y_gla = _gla(proj, gla_w_gate[l], gla_b_gate[l], ret_w + s5_w)
        h = _outproj(h, seq_len, gt2, y_ret.reshape(bsz * seq_len, -1), y_s5.reshape(bsz * seq_len, -1),
                     y_gla.reshape(bsz * seq_len, -1), w_glu, s5_b_glu, w_out, l)
        h = _ffn(h, seq_len, g_ffn2[l], sh3, sc3, gt3, ffn2_w1, ffn2_w3, ffn2_w2, l,
                 g_final if l == depth - 1 else None)
    return h.reshape(bsz, seq_len, d)
```

```python
import functools
import math

import jax
import jax.numpy as jnp
from jax import lax
from jax.experimental import pallas as pl
from jax.experimental.pallas import tpu as pltpu

F32 = jnp.float32
BF16 = jnp.bfloat16
HIGHEST = lax.Precision.HIGHEST

RET_HEADS = 4
RET_DK = 128
RET_DV = 256
RET_CHUNK = 256
ROPE_BASE = 10000.0
S5_GROUP = 16
S5_STATE = 64
S5_CHUNK = 32
GLA_HEADS = 4
GLA_DK = 64
GLA_DV = 128
GLA_RANK = 16
GLA_TAU = 16.0
GLA_CHUNK = 64
GLA_BLOCK = 256
FFN_RES = 0.5
FFN_ROWS = 512
N_MOD = 9
EPS = 1e-6

LANES = 128
MXU_N = 256
VMEM_LIMIT = 60 << 20

NT_DIMS = (((1,), (1,)), ((), ()))
TN_DIMS = (((0,), (0,)), ((), ()))


def _params(*semantics):
    return pltpu.CompilerParams(dimension_semantics=semantics, vmem_limit_bytes=VMEM_LIMIT)


def _sigmoid(x):
    return 1.0 / (1.0 + jnp.exp(-x))


def _norm_mod(x, g, shift, scale):
    ms = jnp.mean(x * x, axis=-1, keepdims=True)
    return x * lax.rsqrt(ms + EPS) * (g * (1.0 + scale)) + shift


def _ada_kernel(c_ref, w_ref, b_ref, o_ref):
    c = c_ref[...]
    cond = c * _sigmoid(c)
    o_ref[0] = jnp.dot(cond, w_ref[0], preferred_element_type=F32, precision=HIGHEST) + b_ref[0]


def _ada_mod(c, w_ada, b_ada):
    depth, d, n = w_ada.shape
    bsz = c.shape[0]
    rows = -(-bsz // 8) * 8
    c_pad = jnp.zeros((rows, d), F32).at[:bsz].set(c)
    tn = n // 16
    out = pl.pallas_call(
        _ada_kernel,
        grid=(depth, n // tn),
        in_specs=[
            pl.BlockSpec((rows, d), lambda l, j: (0, 0)),
            pl.BlockSpec((1, d, tn), lambda l, j: (l, 0, j)),
            pl.BlockSpec((1, 1, tn), lambda l, j: (l, 0, j)),
        ],
        out_specs=pl.BlockSpec((1, rows, tn), lambda l, j: (l, 0, j)),
        out_shape=jax.ShapeDtypeStruct((depth, rows, n), F32),
        compiler_params=_params("parallel", "parallel"),
        name="ada_mod",
    )(c_pad, w_ada, b_ada.reshape(depth, 1, n))
    return out[:, :bsz]


def _ffn_kernel(h_ref, g_ref, sh_ref, sc_ref, gt_ref, w1_ref, w3_ref, w2_ref, *rest, tf, nj, final):
    if final:
        gf_ref, o_ref, u_scr = rest
    else:
        o_ref, u_scr = rest
    j = pl.program_id(1)

    row_blocks = [slice(r0, r0 + FFN_ROWS) for r0 in range(0, o_ref.shape[0], FFN_ROWS)]

    @pl.when(j == 0)
    def _():
        for rows in row_blocks:
            u_scr[rows, :] = _norm_mod(h_ref[rows, :], g_ref[...], sh_ref[0], sc_ref[0]).astype(BF16)
        o_ref[...] = jnp.zeros_like(o_ref)

    for rows in row_blocks:
        u = u_scr[rows, :]
        acts = []
        for c0 in range(0, tf, MXU_N):
            a = jnp.dot(u, w1_ref[:, c0:c0 + MXU_N], preferred_element_type=F32)
            b = jnp.dot(u, w3_ref[:, c0:c0 + MXU_N], preferred_element_type=F32)
            acts.append((a * _sigmoid(a) * b).astype(BF16))
        act = jnp.concatenate(acts, axis=1)
        o_ref[rows, :] += jnp.dot(act, w2_ref[...], preferred_element_type=F32)

    @pl.when(j == nj - 1)
    def _():
        for rows in row_blocks:
            out = h_ref[rows, :] + (FFN_RES * gt_ref[0]) * o_ref[rows, :]
            if final:
                ms = jnp.mean(out * out, axis=-1, keepdims=True)
                out = out * lax.rsqrt(ms + EPS) * gf_ref[...]
            o_ref[rows, :] = out


def _ffn(h, seq_len, g, shift, scale, gate, w1, w3, w2, layer, g_final=None, *, tm=1024, tf=512):
    m, d = h.shape
    dff = w1.shape[2]
    tm = min(tm, seq_len)
    nj = dff // tf
    per_batch = seq_len // tm
    final = g_final is not None
    vec = lambda i, j: (i // per_batch, 0, 0)
    in_specs = [
        pl.BlockSpec((tm, d), lambda i, j: (i, 0)),
        pl.BlockSpec((1, d), lambda i, j: (0, 0)),
        pl.BlockSpec((1, 1, d), vec),
        pl.BlockSpec((1, 1, d), vec),
        pl.BlockSpec((1, 1, d), vec),
        pl.BlockSpec((None, d, tf), lambda i, j: (layer, 0, j)),
        pl.BlockSpec((None, d, tf), lambda i, j: (layer, 0, j)),
        pl.BlockSpec((None, tf, d), lambda i, j: (layer, j, 0)),
    ]
    args = [h, g.reshape(1, d), shift, scale, gate, w1, w3, w2]
    if final:
        in_specs.append(pl.BlockSpec((1, d), lambda i, j: (0, 0)))
        args.append(g_final.reshape(1, d))
    return pl.pallas_call(
        functools.partial(_ffn_kernel, tf=tf, nj=nj, final=final),
        grid=(m // tm, nj),
        in_specs=in_specs,
        out_specs=pl.BlockSpec((tm, d), lambda i, j: (i, 0)),
        out_shape=jax.ShapeDtypeStruct((m, d), F32),
        scratch_shapes=[pltpu.VMEM((tm, d), BF16)],
        compiler_params=_params("parallel", "arbitrary"),
        name="ffn",
    )(*args)


def _inproj_kernel(h_ref, g_ref, sh_ref, sc_ref, w_ref, wst_ref, o_ref, ost_ref):
    u = _norm_mod(h_ref[...], g_ref[...], sh_ref[0], sc_ref[0]).astype(BF16)
    o_ref[...] = jnp.dot(u, w_ref[...], preferred_element_type=F32)
    ost_ref[...] = lax.dot_general(wst_ref[...], u, NT_DIMS, preferred_element_type=F32)


def _inproj(h, seq_len, g, shift, scale, w, w_s5_t, layer, *, tm=256):
    m, d = h.shape
    d_pad = w.shape[2]
    w_s5 = w_s5_t.shape[1]
    tm = min(tm, seq_len)
    per_batch = seq_len // tm
    vec = lambda i: (i // per_batch, 0, 0)
    return pl.pallas_call(
        _inproj_kernel,
        grid=(m // tm,),
        in_specs=[
            pl.BlockSpec((tm, d), lambda i: (i, 0)),
            pl.BlockSpec((1, d), lambda i: (0, 0)),
            pl.BlockSpec((1, 1, d), vec),
            pl.BlockSpec((1, 1, d), vec),
            pl.BlockSpec((None, d, d_pad), lambda i: (layer, 0, 0), pipeline_mode=pl.Buffered(1)),
            pl.BlockSpec((None, w_s5, d), lambda i: (layer, 0, 0), pipeline_mode=pl.Buffered(1)),
        ],
        out_specs=[pl.BlockSpec((tm, d_pad), lambda i: (i, 0)),
                   pl.BlockSpec((w_s5, tm), lambda i: (0, i))],
        out_shape=[jax.ShapeDtypeStruct((m, d_pad), F32), jax.ShapeDtypeStruct((w_s5, m), F32)],
        compiler_params=_params("parallel"),
        name="inproj",
    )(h, g.reshape(1, d), shift, scale, w, w_s5_t)


def _ret_tables(seq_len):
    c = RET_CHUNK
    pos = jnp.arange(seq_len, dtype=F32)
    inv_freq = ROPE_BASE ** (-jnp.arange(0, RET_DK, 2, dtype=F32) / RET_DK)
    ang = pos[:, None] * inv_freq[None, :]
    cos, sin = jnp.cos(ang), jnp.sin(ang)
    cs = jnp.concatenate([cos, cos], axis=-1)
    sn = jnp.concatenate([-sin, sin], axis=-1)
    lg = jnp.log1p(-jnp.exp2(-5.0 - jnp.arange(RET_HEADS, dtype=F32)))
    idx = jnp.arange(c, dtype=F32)
    dist = jnp.abs(idx[:, None] - idx[None, :])
    dmat = jnp.exp(dist[None] * lg[:, None, None])
    rows = jnp.stack([c - 1.0 - idx, idx, idx + 1.0, c - idx])
    dvec = jnp.exp(rows[None] * lg[:, None, None])
    dvec = jnp.broadcast_to(dvec[..., None], dvec.shape + (RET_DK,))
    gc = jnp.broadcast_to(jnp.exp(c * lg)[:, None, None], (RET_HEADS, 1, RET_DV))
    return cs, sn, dmat, dvec, gc


def _ret_kernel(q_ref, k_ref, v_ref, g_ref, cs_ref, sn_ref, dmat_ref, dvec_ref, gc_ref, o_ref,
                q_scr, k_scr, qf_scr, qb_scr, kf_scr, kb_scr, of_scr, ob_scr, sf_scr, sb_scr,
                *, nchunks):
    c = RET_CHUNK
    scale = RET_DK ** -0.5
    chunk_rows = lambda n: pl.ds(pl.multiple_of(n * c, c), c)

    def prepare(n, carry):
        rows = chunk_rows(n)
        cs = cs_ref[rows, :]
        sn = sn_ref[rows, :]
        q = q_ref[0, rows, :]
        k = k_ref[0, rows, :]
        q = (q * cs + pltpu.roll(q, RET_DK // 2, 1) * sn) * scale
        k = k * cs + pltpu.roll(k, RET_DK // 2, 1) * sn
        q_scr[rows, :] = q.astype(BF16)
        k_scr[rows, :] = k.astype(BF16)
        kf_scr[rows, :] = (k * dvec_ref[0, 0]).astype(BF16)
        kb_scr[rows, :] = (k * dvec_ref[0, 1]).astype(BF16)
        qf_scr[rows, :] = (q * dvec_ref[0, 2]).astype(BF16)
        qb_scr[rows, :] = (q * dvec_ref[0, 3]).astype(BF16)
        return carry

    lax.fori_loop(0, nchunks, prepare, 0, unroll=4)
    sf_scr[...] = jnp.zeros_like(sf_scr)
    sb_scr[...] = jnp.zeros_like(sb_scr)
    gc = gc_ref[0]

    def recur(n, carry):
        rf = chunk_rows(n)
        vf = v_ref[0, rf, :].astype(BF16)
        scores = lax.dot_general(q_scr[rf, :], k_scr[rf, :], NT_DIMS,
                                 preferred_element_type=F32) * dmat_ref[0]
        s = sf_scr[...]
        of_scr[rf, :] = (jnp.dot(scores.astype(BF16), vf, preferred_element_type=F32)
                         + jnp.dot(qf_scr[rf, :], s.astype(BF16), preferred_element_type=F32))
        sf_scr[...] = gc * s + lax.dot_general(kf_scr[rf, :], vf, TN_DIMS,
                                               preferred_element_type=F32)
        rb = chunk_rows(nchunks - 1 - n)
        vb = v_ref[0, rb, :].astype(BF16)
        s = sb_scr[...]
        ob_scr[rb, :] = jnp.dot(qb_scr[rb, :], s.astype(BF16), preferred_element_type=F32)
        sb_scr[...] = gc * s + lax.dot_general(kb_scr[rb, :], vb, TN_DIMS,
                                               preferred_element_type=F32)
        return carry

    lax.fori_loop(0, nchunks, recur, 0, unroll=4)

    def finish(n, carry):
        rows = chunk_rows(n)
        o = of_scr[rows, :] + ob_scr[rows, :]
        oc = o - jnp.mean(o, axis=-1, keepdims=True)
        ln = oc * lax.rsqrt(jnp.mean(oc * oc, axis=-1, keepdims=True) + EPS)
        g = g_ref[0, rows, :]
        o_ref[0, rows, :] = (g * _sigmoid(g) * ln).astype(o_ref.dtype)
        return carry

    lax.fori_loop(0, nchunks, finish, 0, unroll=4)


def _retention(proj, tables):
    bsz, seq_len, _ = proj.shape
    cs, sn, dmat, dvec, gc = tables
    h = RET_HEADS
    kq, kk = 0, h
    kv, kg = (2 * h * RET_DK) // RET_DV, (2 * h * RET_DK + h * RET_DV) // RET_DV
    tok = lambda off: (lambda b, i: (b, 0, off + i))
    return pl.pallas_call(
        functools.partial(_ret_kernel, nchunks=seq_len // RET_CHUNK),
        grid=(bsz, h),
        in_specs=[
            pl.BlockSpec((1, seq_len, RET_DK), tok(kq)),
            pl.BlockSpec((1, seq_len, RET_DK), tok(kk)),
            pl.BlockSpec((1, seq_len, RET_DV), tok(kv)),
            pl.BlockSpec((1, seq_len, RET_DV), tok(kg)),
            pl.BlockSpec((seq_len, RET_DK), lambda b, i: (0, 0), pipeline_mode=pl.Buffered(1)),
            pl.BlockSpec((seq_len, RET_DK), lambda b, i: (0, 0), pipeline_mode=pl.Buffered(1)),
            pl.BlockSpec((1, RET_CHUNK, RET_CHUNK), lambda b, i: (i, 0, 0)),
            pl.BlockSpec((1, 4, RET_CHUNK, RET_DK), lambda b, i: (i, 0, 0, 0)),
            pl.BlockSpec((1, 1, RET_DV), lambda b, i: (i, 0, 0)),
        ],
        out_specs=pl.BlockSpec((1, seq_len, RET_DV), lambda b, i: (b, 0, i)),
        out_shape=jax.ShapeDtypeStruct((bsz, seq_len, h * RET_DV), BF16),
        scratch_shapes=(
            [pltpu.VMEM((seq_len, RET_DK), BF16)] * 6
            + [pltpu.VMEM((seq_len, RET_DV), F32)] * 2
            + [pltpu.VMEM((RET_DK, RET_DV), F32)] * 2
        ),
        compiler_params=_params("parallel", "parallel"),
        name="retention",
    )(proj, proj, proj, proj, cs, sn, dmat, dvec, gc)


def _gla_kernel(q_ref, k_ref, v_ref, g_ref, lr_ref, wf_ref, wb_ref, bf_ref, bb_ref, o_ref,
                q0_scr, q1_scr, kin_scr, kst_scr, dec_scr, o_scr, st_scr, *, nblocks):
    c, r = GLA_CHUNK, GLA_BLOCK
    per = r // c
    dk2, dv2 = 2 * GLA_DK, 2 * GLA_DV
    scale = GLA_DK ** -0.5
    block_rows = lambda m: pl.ds(pl.multiple_of(m * r, r), r)

    ri = lax.broadcasted_iota(jnp.int32, (r, r), 0)
    ci = lax.broadcasted_iota(jnp.int32, (r, r), 1)
    same_chunk = (ri // c) == (ci // c)
    lane = lax.broadcasted_iota(jnp.int32, (r, dk2), 1)
    se = lax.broadcasted_iota(jnp.int32, (dv2, dk2), 0)
    sd = lax.broadcasted_iota(jnp.int32, (dv2, dk2), 1)
    same_head = jnp.where(se < GLA_DV, jnp.where(sd < GLA_DK, 1.0, 0.0),
                          jnp.where(sd >= GLA_DK, 1.0, 0.0)).astype(F32)

    def direction(w_ref, b_ref, reverse):
        if reverse:
            cum = same_chunk & (ci >= ri)
            keep = same_chunk & (ci > ri)
        else:
            cum = same_chunk & (ci <= ri)
            keep = same_chunk & (ci <= ri)
        sum_mat = jnp.concatenate([jnp.where(cum, 1.0, 0.0), jnp.where(same_chunk, 1.0, 0.0)],
                                  axis=0).astype(BF16)

        def prepare(m, carry):
            rows = block_rows(m)
            x = jnp.dot(lr_ref[0, rows, :], w_ref[0], preferred_element_type=F32,
                        precision=HIGHEST) + b_ref[0]
            la = (jnp.minimum(x, 0.0) - jnp.log(1.0 + jnp.exp(-jnp.abs(x)))) * (1.0 / GLA_TAU)
            hi = la.astype(BF16)
            rest = la - hi.astype(F32)
            mid = rest.astype(BF16)
            lo = (rest - mid.astype(F32)).astype(BF16)
            sums = jnp.dot(sum_mat, jnp.concatenate([hi, mid, lo], axis=1),
                           preferred_element_type=F32)
            sums = sums[:, :dk2] + sums[:, dk2:2 * dk2] + sums[:, 2 * dk2:]
            b, total = sums[:r], sums[r:]
            q_in = q_ref[0, rows, :] * scale * jnp.exp(b)
            k = k_ref[0, rows, :]
            q0_scr[rows, :] = jnp.where(lane < GLA_DK, q_in, 0.0).astype(BF16)
            q1_scr[rows, :] = jnp.where(lane >= GLA_DK, q_in, 0.0).astype(BF16)
            kin_scr[rows, :] = (k * jnp.exp(-b)).astype(BF16)
            kst_scr[rows, :] = (k * jnp.exp(total - b)).astype(BF16)
            dec_scr[rows, :] = jnp.exp(total)
            return carry

        lax.fori_loop(0, nblocks, prepare, 0, unroll=4)
        st_scr[...] = jnp.zeros_like(st_scr)

        def recur(t, carry):
            rows = block_rows(nblocks - 1 - t if reverse else t)
            q0 = q0_scr[rows, :]
            q1 = q1_scr[rows, :]
            kin = kin_scr[rows, :]
            kst = kst_scr[rows, :]
            dec = dec_scr[rows, :]
            vb = v_ref[0, rows, :].astype(BF16)
            intra = []
            for hh, qh in enumerate((q0, q1)):
                scores = lax.dot_general(qh, kin, NT_DIMS, preferred_element_type=F32)
                scores = jnp.where(keep, scores, 0.0).astype(BF16)
                intra.append(jnp.dot(scores, vb[:, hh * GLA_DV:(hh + 1) * GLA_DV],
                                     preferred_element_type=F32))
            qin = q0 + q1
            st = st_scr[...]
            inter = [None] * per
            for i in (range(per - 1, -1, -1) if reverse else range(per)):
                sl = slice(i * c, (i + 1) * c)
                inter[i] = lax.dot_general(qin[sl], st.astype(BF16), NT_DIMS,
                                           preferred_element_type=F32)
                kv_t = lax.dot_general(vb[sl], kst[sl], TN_DIMS, preferred_element_type=F32)
                st = st * dec[i * c:i * c + 1, :] + kv_t * same_head
            st_scr[...] = st
            o = jnp.concatenate(intra, axis=1) + jnp.concatenate(inter, axis=0)
            if not reverse:
                o_scr[rows, :] = o
            else:
                o = o + o_scr[rows, :]
                normed = []
                for hh in range(2):
                    oh = o[:, hh * GLA_DV:(hh + 1) * GLA_DV]
                    normed.append(oh * lax.rsqrt(jnp.mean(oh * oh, axis=-1, keepdims=True) + EPS))
                g = g_ref[0, rows, :]
                o_ref[0, rows, :] = (g * _sigmoid(g)
                                     * jnp.concatenate(normed, axis=1)).astype(o_ref.dtype)
            return carry

        lax.fori_loop(0, nblocks, recur, 0, unroll=4)

    direction(wf_ref, bf_ref, False)
    direction(wb_ref, bb_ref, True)


def _gla(proj, w_gate, b_gate, col0):
    bsz, seq_len, _ = proj.shape
    pairs = GLA_HEADS // 2
    dk2, dv2 = 2 * GLA_DK, 2 * GLA_DV
    c_q = col0
    c_k = c_q + GLA_HEADS * GLA_DK
    c_v = c_k + GLA_HEADS * GLA_DK
    c_g = c_v + GLA_HEADS * GLA_DV
    c_lr = c_g + GLA_HEADS * GLA_DV
    wg = w_gate.astype(F32).reshape(2, GLA_RANK, pairs, dk2).transpose(0, 2, 1, 3)
    wf = jnp.zeros((pairs, LANES, dk2), F32).at[:, :GLA_RANK].set(wg[0])
    wb = jnp.zeros((pairs, LANES, dk2), F32).at[:, GLA_RANK:2 * GLA_RANK].set(wg[1])
    bg = b_gate.astype(F32).reshape(2, pairs, 1, dk2)
    tok = lambda col, width: (lambda b, p: (b, 0, col // width + p))
    mat = lambda b, p: (p, 0, 0)
    return pl.pallas_call(
        functools.partial(_gla_kernel, nblocks=seq_len // GLA_BLOCK),
        grid=(bsz, pairs),
        in_specs=[
            pl.BlockSpec((1, seq_len, dk2), tok(c_q, dk2)),
            pl.BlockSpec((1, seq_len, dk2), tok(c_k, dk2)),
            pl.BlockSpec((1, seq_len, dv2), tok(c_v, dv2)),
            pl.BlockSpec((1, seq_len, dv2), tok(c_g, dv2)),
            pl.BlockSpec((1, seq_len, LANES), lambda b, p: (b, 0, c_lr // LANES)),
            pl.BlockSpec((1, LANES, dk2), mat),
            pl.BlockSpec((1, LANES, dk2), mat),
            pl.BlockSpec((1, 1, dk2), mat),
            pl.BlockSpec((1, 1, dk2), mat),
        ],
        out_specs=pl.BlockSpec((1, seq_len, dv2), lambda b, p: (b, 0, p)),
        out_shape=jax.ShapeDtypeStruct((bsz, seq_len, GLA_HEADS * GLA_DV), BF16),
        scratch_shapes=(
            [pltpu.VMEM((seq_len, dk2), BF16)] * 4
            + [pltpu.VMEM((seq_len, dk2), F32), pltpu.VMEM((seq_len, dv2), F32),
               pltpu.VMEM((dv2, dk2), F32)]
        ),
        compiler_params=_params("parallel", "parallel"),
        name="gla",
    )(proj, proj, proj, proj, proj, wf, wb, bg[0], bg[1])


def _swap(x):
    return pltpu.roll(x, S5_STATE, 1)


def _s5_prep_kernel(lre_ref, lim_ref, ldt_ref, brr_ref, bis_ref, cr_ref, ci_ref,
                    m_ref, e_ref, ft_ref, a_ref):
    t_len = S5_CHUNK
    gsz = S5_GROUP
    width = gsz * t_len
    first = lax.broadcasted_iota(jnp.int32, (1, LANES), 1) < S5_STATE
    one_zero = jnp.where(first, 1.0, 0.0).astype(F32)
    sign = jnp.where(first, 1.0, -1.0).astype(F32)

    def packed_exp(t, are, aim):
        ph = t * aim
        return jnp.exp(t * are) * jnp.where(first, jnp.cos(ph), jnp.sin(ph))

    def dup_re(x):
        return jnp.where(first, x, _swap(x))

    def dup_im_signed(x):
        return jnp.where(first, -_swap(x), x)

    def per_channel(pw, w_a, w_b):
        return jnp.concatenate([pw * w_a[c:c + 1, :] + _swap(pw) * w_b[c:c + 1, :]
                                for c in range(gsz)], axis=0)

    trow = lax.broadcasted_iota(jnp.int32, (t_len, 1), 0).astype(F32)
    panels = []
    for d in range(2):
        lre = lre_ref[0, d:d + 1, :]
        lim = lim_ref[0, d:d + 1, :]
        dt = jnp.exp(ldt_ref[0, d:d + 1, :])
        are, aim = lre * dt, lim * dt
        num = packed_exp(jnp.ones((1, 1), F32), are, aim) - one_zero
        inv_den = 1.0 / (lre * lre + lim * lim)
        coef = num * (lre * inv_den) + _swap(num) * (lim * inv_den * sign)
        bbar = coef * brr_ref[0, d] + _swap(coef) * bis_ref[0, d]
        if d == 0:
            t_e, t_f, t_g = t_len - 1.0 - trow, trow + 1.0, trow
        else:
            t_e, t_f, t_g = trow, t_len - trow, t_len - 1.0 - trow
        e_mat = per_channel(packed_exp(t_e, are, aim), dup_re(bbar), dup_im_signed(bbar))
        e_ref[0, :, d * LANES:(d + 1) * LANES] = e_mat.astype(BF16)
        cr, ci = cr_ref[0, d], ci_ref[0, d]
        ft_ref[0, d] = per_channel(packed_exp(t_f, are, aim), cr, ci).astype(BF16)
        g_mat = per_channel(packed_exp(t_g, are, aim), cr, ci)
        panels.append(lax.dot_general(bbar, g_mat, NT_DIMS, preferred_element_type=F32,
                                      precision=HIGHEST))
        kcol = lax.broadcasted_iota(jnp.int32, (8, 1), 0)
        tk = (t_len * jnp.left_shift(1, kcol)).astype(F32)
        ak = packed_exp(tk, are, aim)
        a_ref[0, d, 0:8, :] = dup_re(ak)
        a_ref[0, d, 8:16, :] = dup_im_signed(ak)

    p_fwd, p_bwd = panels
    j = lax.broadcasted_iota(jnp.int32, (t_len, width), 0)
    pos = lax.broadcasted_iota(jnp.int32, (t_len, width), 1) % t_len
    for c in range(gsz):
        fwd = pltpu.roll(jnp.broadcast_to(p_fwd[c:c + 1, :], (t_len, width)), 0, 1,
                         stride=1, stride_axis=0)
        bwd = pltpu.roll(jnp.broadcast_to(p_bwd[c:c + 1, :], (t_len, width)), width - (t_len - 1), 1,
                         stride=1, stride_axis=0)
        blk = jnp.where(pos >= j, fwd, 0.0) + jnp.where(pos <= j, bwd, 0.0)
        m_ref[0, c * t_len:(c + 1) * t_len, :] = blk.astype(BF16)


def _s5_apply_kernel(u_ref, m_ref, e_ref, ft_ref, a_ref, d_ref, y_ref, *, nchunks, nsteps):
    t_len = S5_CHUNK
    u = jnp.concatenate([u_ref[0, c] for c in range(S5_GROUP)], axis=1)
    ub = u.astype(BF16)
    rows = u.shape[0]
    y = jnp.dot(ub, m_ref[0], preferred_element_type=F32) + u * d_ref[0]
    s = jnp.dot(ub, e_ref[0], preferred_element_type=F32)
    nidx = lax.broadcasted_iota(jnp.int32, (rows, LANES), 0) % nchunks
    for d in range(2):
        x = s[:, d * LANES:(d + 1) * LANES]

        def shifted(val, sh):
            if d == 0:
                return jnp.where(nidx >= sh, pltpu.roll(val, sh, 0), 0.0)
            return jnp.where(nidx < nchunks - sh, pltpu.roll(val, rows - sh, 0), 0.0)

        for k in range(nsteps):
            xs = shifted(x, 1 << k)
            x = x + a_ref[0, d, k:k + 1, :] * xs + a_ref[0, d, 8 + k:9 + k, :] * _swap(xs)
        xp = shifted(x, 1).astype(BF16)
        y = y + lax.dot_general(xp, ft_ref[0, d], NT_DIMS, preferred_element_type=F32)
    for c in range(S5_GROUP):
        y_ref[0, c] = y[:, c * t_len:(c + 1) * t_len]


def _s5(su_t, seq_len, lam_re, lam_im, log_dt, b_re, b_im, c_re, c_im, d_skip):
    w, tokens = su_t.shape
    g = w // S5_GROUP
    t_len = S5_CHUNK
    nchunks = seq_len // t_len
    nsteps = max(1, (nchunks - 1).bit_length())
    assert nsteps <= 8
    rows = tokens // t_len
    width = S5_GROUP * t_len
    dup = lambda a: jnp.concatenate([a, a], axis=-1)
    gd = lambda a: jnp.moveaxis(a.astype(F32), 0, 1)
    lre = dup(gd(lam_re))
    lim = dup(gd(lam_im))
    ldt = jnp.broadcast_to(gd(log_dt)[..., None], (g, 2, LANES))
    bt_re = jnp.swapaxes(gd(b_re), -1, -2)
    bt_im = jnp.swapaxes(gd(b_im), -1, -2)
    brr = jnp.concatenate([bt_re, bt_re], axis=-1)
    bis = jnp.concatenate([-bt_im, bt_im], axis=-1)
    cre, cim = gd(c_re), gd(c_im)
    cr = jnp.concatenate([cre, -cre], axis=-1)
    ci = jnp.concatenate([-cim, -cim], axis=-1)
    vec3 = pl.BlockSpec((1, 2, LANES), lambda i: (i, 0, 0))
    mat4 = pl.BlockSpec((1, 2, S5_GROUP, LANES), lambda i: (i, 0, 0, 0))
    m_mat, e_mat, ft_mat, a_mat = pl.pallas_call(
        _s5_prep_kernel,
        grid=(g,),
        in_specs=[vec3, vec3, vec3, mat4, mat4, mat4, mat4],
        out_specs=[
            pl.BlockSpec((1, width, width), lambda i: (i, 0, 0)),
            pl.BlockSpec((1, width, 2 * LANES), lambda i: (i, 0, 0)),
            pl.BlockSpec((1, 2, width, LANES), lambda i: (i, 0, 0, 0)),
            pl.BlockSpec((1, 2, 16, LANES), lambda i: (i, 0, 0, 0)),
        ],
        out_shape=[
            jax.ShapeDtypeStruct((g, width, width), BF16),
            jax.ShapeDtypeStruct((g, width, 2 * LANES), BF16),
            jax.ShapeDtypeStruct((g, 2, width, LANES), BF16),
            jax.ShapeDtypeStruct((g, 2, 16, LANES), F32),
        ],
        compiler_params=_params("parallel"),
        name="s5_prep",
    )(lre, lim, ldt, brr, bis, cr, ci)
    d_rep = jnp.repeat(d_skip.astype(F32).reshape(g, 1, S5_GROUP), t_len, axis=-1)
    blk = pl.BlockSpec((1, S5_GROUP, rows, t_len), lambda i: (i, 0, 0, 0))
    y_t = pl.pallas_call(
        functools.partial(_s5_apply_kernel, nchunks=nchunks, nsteps=nsteps),
        grid=(g,),
        in_specs=[
            blk,
            pl.BlockSpec((1, width, width), lambda i: (i, 0, 0)),
            pl.BlockSpec((1, width, 2 * LANES), lambda i: (i, 0, 0)),
            pl.BlockSpec((1, 2, width, LANES), lambda i: (i, 0, 0, 0)),
            pl.BlockSpec((1, 2, 16, LANES), lambda i: (i, 0, 0, 0)),
            pl.BlockSpec((1, 1, width), lambda i: (i, 0, 0)),
        ],
        out_specs=blk,
        out_shape=jax.ShapeDtypeStruct((g, S5_GROUP, rows, t_len), F32),
        compiler_params=_params("parallel"),
        name="s5_apply",
    )(su_t.reshape(g, S5_GROUP, rows, t_len), m_mat, e_mat, ft_mat, a_mat, d_rep)
    return y_t.reshape(w, tokens)


def _outproj_kernel(h_ref, gt_ref, yr_ref, ys_ref, yg_ref, wglu_ref, bglu_ref, wrg_ref, ws_ref, o_ref):
    y = ys_ref[...]
    z = y * (0.5 * (1.0 + jnp.tanh(math.sqrt(2.0 / math.pi) * (y + 0.044715 * (y * y * y)))))
    gl = jnp.dot(wglu_ref[...], z.astype(BF16), preferred_element_type=F32) + bglu_ref[...]
    s5 = (z * _sigmoid(gl)).astype(BF16)
    acc = jnp.dot(jnp.concatenate([yr_ref[...], yg_ref[...]], axis=1), wrg_ref[...],
                  preferred_element_type=F32)
    acc = acc + lax.dot_general(s5, ws_ref[...], TN_DIMS, preferred_element_type=F32)
    o_ref[...] = h_ref[...] + gt_ref[0] * acc


def _outproj(h, seq_len, gate, y_ret, y_s5_t, y_gla, wglu_t, b_glu, w_rg, w_s, layer, *, tm=512):
    m, d = h.shape
    tm = min(tm, seq_len)
    per_batch = seq_len // tm
    w_s5 = y_s5_t.shape[0]
    row = lambda width: pl.BlockSpec((tm, width), lambda i: (i, 0))
    whole = lambda a: pl.BlockSpec((None,) + a.shape[1:], lambda i: (layer, 0, 0),
                                   pipeline_mode=pl.Buffered(1))
    bglu = b_glu.astype(F32).reshape(-1, w_s5, 1)
    return pl.pallas_call(
        _outproj_kernel,
        grid=(m // tm,),
        in_specs=[
            row(d),
            pl.BlockSpec((1, 1, d), lambda i: (i // per_batch, 0, 0)),
            row(y_ret.shape[1]),
            pl.BlockSpec((w_s5, tm), lambda i: (0, i)),
            row(y_gla.shape[1]),
            whole(wglu_t), whole(bglu), whole(w_rg), whole(w_s),
        ],
        out_specs=row(d),
        out_shape=jax.ShapeDtypeStruct((m, d), F32),
        compiler_params=_params("parallel"),
        name="outproj",
    )(h, gate, y_ret, y_s5_t, y_gla, wglu_t, bglu, w_rg, w_s)


def kernel(x, c, w_ada, b_ada, g_ffn1, ffn1_w1, ffn1_w3, ffn1_w2, g_mix, w_in, s5_lam_re, s5_lam_im, s5_log_dt, s5_b_re, s5_b_im, s5_c_re, s5_c_im, s5_d, s5_w_glu, s5_b_glu, gla_w_gate, gla_b_gate, w_out, g_ffn2, ffn2_w1, ffn2_w3, ffn2_w2, g_final):
    bsz, seq_len, d = x.shape
    depth = w_ada.shape[0]
    mod = _ada_mod(c, w_ada, b_ada).reshape(depth, bsz, N_MOD, 1, d)
    tables = _ret_tables(seq_len)
    ret_w = RET_HEADS * (2 * RET_DK + 2 * RET_DV)
    s5_w = s5_d.shape[1]
    h = x.reshape(bsz * seq_len, d)
    ffn1_w1, ffn1_w3, ffn1_w2, ffn2_w1, ffn2_w3, ffn2_w2 = (
        w.astype(BF16) for w in (ffn1_w1, ffn1_w3, ffn1_w2, ffn2_w1, ffn2_w3, ffn2_w2))
    w_in = w_in.astype(BF16)
    w_tok = jnp.concatenate([w_in[:, :, :ret_w], w_in[:, :, ret_w + s5_w:]], axis=2)
    w_tok = jnp.pad(w_tok, ((0, 0), (0, 0), (0, -w_tok.shape[2] % LANES)))
    w_s5_t = jnp.swapaxes(w_in[:, :, ret_w:ret_w + s5_w], 1, 2)
    w_out = w_out.astype(BF16)
    ret_o = RET_HEADS * RET_DV
    w_rg = jnp.concatenate([w_out[:, :ret_o], w_out[:, ret_o + s5_w:]], axis=1)
    w_s = w_out[:, ret_o:ret_o + s5_w]
    w_glu_t = jnp.swapaxes(s5_w_glu.astype(BF16), 1, 2)
    for l in range(depth):
        sh1, sc1, gt1, sh2, sc2, gt2, sh3, sc3, gt3 = (mod[l, :, i] for i in range(N_MOD))
        h = _ffn(h, seq_len, g_ffn1[l], sh1, sc1, gt1, ffn1_w1, ffn1_w3, ffn1_w2, l)
        proj, su_t = _inproj(h, seq_len, g_mix[l], sh2, sc2, w_tok, w_s5_t, l)
        proj = proj.reshape(bsz, seq_len, -1)
        y_ret = _retention(proj, tables)
        y_s5_t = _s5(su_t, seq_len, s5_lam_re[l], s5_lam_im[l], s5_log_dt[l],
                     s5_b_re[l], s5_b_im[l], s5_c_re[l], s5_c_im[l], s5_d[l])
        y_gla = _gla(proj, gla_w_gate[l], gla_b_gate[l], ret_w)
        h = _outproj(h, seq_len, gt2, y_ret.reshape(bsz * seq_len, -1), y_s5_t,
                     y_gla.reshape(bsz * seq_len, -1), w_glu_t, s5_b_glu, w_rg, w_s, l)
        h = _ffn(h, seq_len, g_ffn2[l], sh3, sc3, gt3, ffn2_w1, ffn2_w3, ffn2_w2, l,
                 g_final if l == depth - 1 else None)
    return h.reshape(bsz, seq_len, d)
```

```python
import functools
import math

import jax
import jax.numpy as jnp
from jax import lax
from jax.experimental import pallas as pl
from jax.experimental.pallas import tpu as pltpu

F32 = jnp.float32
BF16 = jnp.bfloat16
HIGHEST = lax.Precision.HIGHEST

RET_HEADS = 4
RET_DK = 128
RET_DV = 256
RET_CHUNK = 256
ROPE_BASE = 10000.0
S5_GROUP = 16
S5_STATE = 64
S5_CHUNK = 32
GLA_HEADS = 4
GLA_DK = 64
GLA_DV = 128
GLA_RANK = 16
GLA_TAU = 16.0
GLA_CHUNK = 64
GLA_BLOCK = 256
FFN_RES = 0.5
FFN_ROWS = 512
N_MOD = 9
EPS = 1e-6

LANES = 128
SUBLANES = 8
MXU_N = 256
VMEM_LIMIT = 60 << 20

NT_DIMS = (((1,), (1,)), ((), ()))
TN_DIMS = (((0,), (0,)), ((), ()))


def _params(*semantics):
    return pltpu.CompilerParams(dimension_semantics=semantics, vmem_limit_bytes=VMEM_LIMIT)


def _sigmoid(x):
    return 1.0 / (1.0 + jnp.exp(-x))


def _norm_mod(x, g, shift, scale):
    ms = jnp.mean(x * x, axis=-1, keepdims=True)
    return x * lax.rsqrt(ms + EPS) * (g * (1.0 + scale)) + shift


def _ada_kernel(c_ref, w_ref, b_ref, o_ref):
    c = c_ref[...]
    cond = c * _sigmoid(c)
    o_ref[0] = jnp.dot(cond, w_ref[0], preferred_element_type=F32, precision=HIGHEST) + b_ref[0]


def _ada_mod(c, w_ada, b_ada):
    depth, d, n = w_ada.shape
    bsz = c.shape[0]
    rows = -(-bsz // 8) * 8
    c_pad = jnp.zeros((rows, d), F32).at[:bsz].set(c)
    tn = n // 16
    out = pl.pallas_call(
        _ada_kernel,
        grid=(depth, n // tn),
        in_specs=[
            pl.BlockSpec((rows, d), lambda l, j: (0, 0)),
            pl.BlockSpec((1, d, tn), lambda l, j: (l, 0, j)),
            pl.BlockSpec((1, 1, tn), lambda l, j: (l, 0, j)),
        ],
        out_specs=pl.BlockSpec((1, rows, tn), lambda l, j: (l, 0, j)),
        out_shape=jax.ShapeDtypeStruct((depth, rows, n), F32),
        compiler_params=_params("parallel", "parallel"),
        name="ada_mod",
    )(c_pad, w_ada, b_ada.reshape(depth, 1, n))
    return out[:, :bsz]


def _ffn_kernel(h_ref, g_ref, sh_ref, sc_ref, gt_ref, w1_ref, w3_ref, w2_ref, *rest, tf, nj, final):
    if final:
        gf_ref, o_ref, u_scr = rest
    else:
        o_ref, u_scr = rest
    j = pl.program_id(1)

    row_blocks = [slice(r0, r0 + FFN_ROWS) for r0 in range(0, o_ref.shape[0], FFN_ROWS)]

    @pl.when(j == 0)
    def _():
        for rows in row_blocks:
            u_scr[rows, :] = _norm_mod(h_ref[rows, :], g_ref[...], sh_ref[0], sc_ref[0]).astype(BF16)
        o_ref[...] = jnp.zeros_like(o_ref)

    for rows in row_blocks:
        u = u_scr[rows, :]
        acts = []
        for c0 in range(0, tf, MXU_N):
            a = jnp.dot(u, w1_ref[:, c0:c0 + MXU_N], preferred_element_type=F32)
            b = jnp.dot(u, w3_ref[:, c0:c0 + MXU_N], preferred_element_type=F32)
            acts.append((a * _sigmoid(a) * b).astype(BF16))
        act = jnp.concatenate(acts, axis=1)
        o_ref[rows, :] += jnp.dot(act, w2_ref[...], preferred_element_type=F32)

    @pl.when(j == nj - 1)
    def _():
        for rows in row_blocks:
            out = h_ref[rows, :] + (FFN_RES * gt_ref[0]) * o_ref[rows, :]
            if final:
                ms = jnp.mean(out * out, axis=-1, keepdims=True)
                out = out * lax.rsqrt(ms + EPS) * gf_ref[...]
            o_ref[rows, :] = out


def _ffn(h, seq_len, g, shift, scale, gate, w1, w3, w2, layer, g_final=None, *, tm=1024, tf=512):
    m, d = h.shape
    dff = w1.shape[2]
    tm = min(tm, seq_len)
    nj = dff // tf
    per_batch = seq_len // tm
    final = g_final is not None
    vec = lambda i, j: (i // per_batch, 0, 0)
    in_specs = [
        pl.BlockSpec((tm, d), lambda i, j: (i, 0)),
        pl.BlockSpec((1, d), lambda i, j: (0, 0)),
        pl.BlockSpec((1, 1, d), vec),
        pl.BlockSpec((1, 1, d), vec),
        pl.BlockSpec((1, 1, d), vec),
        pl.BlockSpec((None, d, tf), lambda i, j: (layer, 0, j)),
        pl.BlockSpec((None, d, tf), lambda i, j: (layer, 0, j)),
        pl.BlockSpec((None, tf, d), lambda i, j: (layer, j, 0)),
    ]
    args = [h, g.reshape(1, d), shift, scale, gate, w1, w3, w2]
    if final:
        in_specs.append(pl.BlockSpec((1, d), lambda i, j: (0, 0)))
        args.append(g_final.reshape(1, d))
    return pl.pallas_call(
        functools.partial(_ffn_kernel, tf=tf, nj=nj, final=final),
        grid=(m // tm, nj),
        in_specs=in_specs,
        out_specs=pl.BlockSpec((tm, d), lambda i, j: (i, 0)),
        out_shape=jax.ShapeDtypeStruct((m, d), F32),
        scratch_shapes=[pltpu.VMEM((tm, d), BF16)],
        compiler_params=_params("parallel", "arbitrary"),
        name="ffn",
    )(*args)


def _inproj_kernel(h_ref, g_ref, sh_ref, sc_ref, w_ref, wst_ref, o_ref, ost_ref, *, steps):
    u = _norm_mod(h_ref[...], g_ref[...], sh_ref[0], sc_ref[0]).astype(BF16)
    o_ref[...] = jnp.dot(u, w_ref[...], preferred_element_type=F32)
    su_t = lax.dot_general(wst_ref[...], u, NT_DIMS, preferred_element_type=F32)
    sub = su_t.shape[1] // LANES
    base = (pl.program_id(0) % steps) * sub
    for k in range(sub):
        ost_ref[:, pl.ds(base + k, 1), :] = su_t[:, k * LANES:(k + 1) * LANES][:, None, :]


def _inproj(h, seq_len, g, shift, scale, w, w_s5_t, layer, *, tm=256):
    m, d = h.shape
    d_pad = w.shape[2]
    w_s5 = w_s5_t.shape[1]
    tm = min(tm, seq_len)
    per_batch = seq_len // tm
    steps = SUBLANES * LANES // tm
    vec = lambda i: (i // per_batch, 0, 0)
    return pl.pallas_call(
        functools.partial(_inproj_kernel, steps=steps),
        grid=(m // tm,),
        in_specs=[
            pl.BlockSpec((tm, d), lambda i: (i, 0)),
            pl.BlockSpec((1, d), lambda i: (0, 0)),
            pl.BlockSpec((1, 1, d), vec),
            pl.BlockSpec((1, 1, d), vec),
            pl.BlockSpec((None, d, d_pad), lambda i: (layer, 0, 0), pipeline_mode=pl.Buffered(1)),
            pl.BlockSpec((None, w_s5, d), lambda i: (layer, 0, 0), pipeline_mode=pl.Buffered(1)),
        ],
        out_specs=[pl.BlockSpec((tm, d_pad), lambda i: (i, 0)),
                   pl.BlockSpec((w_s5, SUBLANES, LANES), lambda i: (0, i // steps, 0))],
        out_shape=[jax.ShapeDtypeStruct((m, d_pad), F32),
                   jax.ShapeDtypeStruct((w_s5, m // LANES, LANES), F32)],
        compiler_params=_params("arbitrary"),
        name="inproj",
    )(h, g.reshape(1, d), shift, scale, w, w_s5_t)


def _ret_tables(seq_len):
    c = RET_CHUNK
    pos = jnp.arange(seq_len, dtype=F32)
    inv_freq = ROPE_BASE ** (-jnp.arange(0, RET_DK, 2, dtype=F32) / RET_DK)
    ang = pos[:, None] * inv_freq[None, :]
    cos, sin = jnp.cos(ang), jnp.sin(ang)
    cs = jnp.concatenate([cos, cos], axis=-1)
    sn = jnp.concatenate([-sin, sin], axis=-1)
    lg = jnp.log1p(-jnp.exp2(-5.0 - jnp.arange(RET_HEADS, dtype=F32)))
    idx = jnp.arange(c, dtype=F32)
    dist = jnp.abs(idx[:, None] - idx[None, :])
    dmat = jnp.exp(dist[None] * lg[:, None, None])
    rows = jnp.stack([c - 1.0 - idx, idx, idx + 1.0, c - idx])
    dvec = jnp.exp(rows[None] * lg[:, None, None])
    dvec = jnp.broadcast_to(dvec[..., None], dvec.shape + (RET_DK,))
    gc = jnp.broadcast_to(jnp.exp(c * lg)[:, None, None], (RET_HEADS, 1, RET_DV))
    return cs, sn, dmat, dvec, gc


def _ret_kernel(q_ref, k_ref, v_ref, g_ref, cs_ref, sn_ref, dmat_ref, dvec_ref, gc_ref, o_ref,
                q_scr, k_scr, qf_scr, qb_scr, kf_scr, kb_scr, of_scr, ob_scr, sf_scr, sb_scr,
                *, nchunks):
    c = RET_CHUNK
    scale = RET_DK ** -0.5
    chunk_rows = lambda n: pl.ds(pl.multiple_of(n * c, c), c)

    def prepare(n, carry):
        rows = chunk_rows(n)
        cs = cs_ref[rows, :]
        sn = sn_ref[rows, :]
        q = q_ref[0, rows, :]
        k = k_ref[0, rows, :]
        q = (q * cs + pltpu.roll(q, RET_DK // 2, 1) * sn) * scale
        k = k * cs + pltpu.roll(k, RET_DK // 2, 1) * sn
        q_scr[rows, :] = q.astype(BF16)
        k_scr[rows, :] = k.astype(BF16)
        kf_scr[rows, :] = (k * dvec_ref[0, 0]).astype(BF16)
        kb_scr[rows, :] = (k * dvec_ref[0, 1]).astype(BF16)
        qf_scr[rows, :] = (q * dvec_ref[0, 2]).astype(BF16)
        qb_scr[rows, :] = (q * dvec_ref[0, 3]).astype(BF16)
        return carry

    lax.fori_loop(0, nchunks, prepare, 0, unroll=4)
    sf_scr[...] = jnp.zeros_like(sf_scr)
    sb_scr[...] = jnp.zeros_like(sb_scr)
    gc = gc_ref[0]

    def recur(n, carry):
        rf = chunk_rows(n)
        vf = v_ref[0, rf, :].astype(BF16)
        scores = lax.dot_general(q_scr[rf, :], k_scr[rf, :], NT_DIMS,
                                 preferred_element_type=F32) * dmat_ref[0]
        s = sf_scr[...]
        of_scr[rf, :] = (jnp.dot(scores.astype(BF16), vf, preferred_element_type=F32)
                         + jnp.dot(qf_scr[rf, :], s.astype(BF16), preferred_element_type=F32))
        sf_scr[...] = gc * s + lax.dot_general(kf_scr[rf, :], vf, TN_DIMS,
                                               preferred_element_type=F32)
        rb = chunk_rows(nchunks - 1 - n)
        vb = v_ref[0, rb, :].astype(BF16)
        s = sb_scr[...]
        ob_scr[rb, :] = jnp.dot(qb_scr[rb, :], s.astype(BF16), preferred_element_type=F32)
        sb_scr[...] = gc * s + lax.dot_general(kb_scr[rb, :], vb, TN_DIMS,
                                               preferred_element_type=F32)
        return carry

    lax.fori_loop(0, nchunks, recur, 0, unroll=4)

    def finish(n, carry):
        rows = chunk_rows(n)
        o = of_scr[rows, :] + ob_scr[rows, :]
        oc = o - jnp.mean(o, axis=-1, keepdims=True)
        ln = oc * lax.rsqrt(jnp.mean(oc * oc, axis=-1, keepdims=True) + EPS)
        g = g_ref[0, rows, :]
        o_ref[0, rows, :] = (g * _sigmoid(g) * ln).astype(o_ref.dtype)
        return carry

    lax.fori_loop(0, nchunks, finish, 0, unroll=4)


def _retention(proj, tables):
    bsz, seq_len, _ = proj.shape
    cs, sn, dmat, dvec, gc = tables
    h = RET_HEADS
    kq, kk = 0, h
    kv, kg = (2 * h * RET_DK) // RET_DV, (2 * h * RET_DK + h * RET_DV) // RET_DV
    tok = lambda off: (lambda b, i: (b, 0, off + i))
    return pl.pallas_call(
        functools.partial(_ret_kernel, nchunks=seq_len // RET_CHUNK),
        grid=(bsz, h),
        in_specs=[
            pl.BlockSpec((1, seq_len, RET_DK), tok(kq)),
            pl.BlockSpec((1, seq_len, RET_DK), tok(kk)),
            pl.BlockSpec((1, seq_len, RET_DV), tok(kv)),
            pl.BlockSpec((1, seq_len, RET_DV), tok(kg)),
            pl.BlockSpec((seq_len, RET_DK), lambda b, i: (0, 0), pipeline_mode=pl.Buffered(1)),
            pl.BlockSpec((seq_len, RET_DK), lambda b, i: (0, 0), pipeline_mode=pl.Buffered(1)),
            pl.BlockSpec((1, RET_CHUNK, RET_CHUNK), lambda b, i: (i, 0, 0)),
            pl.BlockSpec((1, 4, RET_CHUNK, RET_DK), lambda b, i: (i, 0, 0, 0)),
            pl.BlockSpec((1, 1, RET_DV), lambda b, i: (i, 0, 0)),
        ],
        out_specs=pl.BlockSpec((1, seq_len, RET_DV), lambda b, i: (b, 0, i)),
        out_shape=jax.ShapeDtypeStruct((bsz, seq_len, h * RET_DV), BF16),
        scratch_shapes=(
            [pltpu.VMEM((seq_len, RET_DK), BF16)] * 6
            + [pltpu.VMEM((seq_len, RET_DV), F32)] * 2
            + [pltpu.VMEM((RET_DK, RET_DV), F32)] * 2
        ),
        compiler_params=_params("parallel", "parallel"),
        name="retention",
    )(proj, proj, proj, proj, cs, sn, dmat, dvec, gc)


def _gla_kernel(q_ref, k_ref, v_ref, g_ref, lr_ref, wf_ref, wb_ref, bf_ref, bb_ref, o_ref,
                q0_scr, q1_scr, kin_scr, kst_scr, dec_scr, o_scr, st_scr, *, nblocks):
    c, r = GLA_CHUNK, GLA_BLOCK
    per = r // c
    dk2, dv2 = 2 * GLA_DK, 2 * GLA_DV
    scale = GLA_DK ** -0.5
    block_rows = lambda m: pl.ds(pl.multiple_of(m * r, r), r)

    ri = lax.broadcasted_iota(jnp.int32, (r, r), 0)
    ci = lax.broadcasted_iota(jnp.int32, (r, r), 1)
    same_chunk = (ri // c) == (ci // c)
    lane = lax.broadcasted_iota(jnp.int32, (r, dk2), 1)
    se = lax.broadcasted_iota(jnp.int32, (dv2, dk2), 0)
    sd = lax.broadcasted_iota(jnp.int32, (dv2, dk2), 1)
    same_head = jnp.where(se < GLA_DV, jnp.where(sd < GLA_DK, 1.0, 0.0),
                          jnp.where(sd >= GLA_DK, 1.0, 0.0)).astype(F32)

    def direction(w_ref, b_ref, reverse):
        if reverse:
            cum = same_chunk & (ci >= ri)
            keep = same_chunk & (ci > ri)
        else:
            cum = same_chunk & (ci <= ri)
            keep = same_chunk & (ci <= ri)
        sum_mat = jnp.concatenate([jnp.where(cum, 1.0, 0.0), jnp.where(same_chunk, 1.0, 0.0)],
                                  axis=0).astype(BF16)

        def prepare(m, carry):
            rows = block_rows(m)
            x = jnp.dot(lr_ref[0, rows, :], w_ref[0], preferred_element_type=F32,
                        precision=HIGHEST) + b_ref[0]
            la = (jnp.minimum(x, 0.0) - jnp.log(1.0 + jnp.exp(-jnp.abs(x)))) * (1.0 / GLA_TAU)
            hi = la.astype(BF16)
            rest = la - hi.astype(F32)
            mid = rest.astype(BF16)
            lo = (rest - mid.astype(F32)).astype(BF16)
            sums = jnp.dot(sum_mat, jnp.concatenate([hi, mid, lo], axis=1),
                           preferred_element_type=F32)
            sums = sums[:, :dk2] + sums[:, dk2:2 * dk2] + sums[:, 2 * dk2:]
            b, total = sums[:r], sums[r:]
            q_in = q_ref[0, rows, :] * scale * jnp.exp(b)
            k = k_ref[0, rows, :]
            q0_scr[rows, :] = jnp.where(lane < GLA_DK, q_in, 0.0).astype(BF16)
            q1_scr[rows, :] = jnp.where(lane >= GLA_DK, q_in, 0.0).astype(BF16)
            kin_scr[rows, :] = (k * jnp.exp(-b)).astype(BF16)
            kst_scr[rows, :] = (k * jnp.exp(total - b)).astype(BF16)
            dec_scr[rows, :] = jnp.exp(total)
            return carry

        lax.fori_loop(0, nblocks, prepare, 0, unroll=4)
        st_scr[...] = jnp.zeros_like(st_scr)

        def recur(t, carry):
            rows = block_rows(nblocks - 1 - t if reverse else t)
            q0 = q0_scr[rows, :]
            q1 = q1_scr[rows, :]
            kin = kin_scr[rows, :]
            kst = kst_scr[rows, :]
            dec = dec_scr[rows, :]
            vb = v_ref[0, rows, :].astype(BF16)
            intra = []
            for hh, qh in enumerate((q0, q1)):
                scores = lax.dot_general(qh, kin, NT_DIMS, preferred_element_type=F32)
                scores = jnp.where(keep, scores, 0.0).astype(BF16)
                intra.append(jnp.dot(scores, vb[:, hh * GLA_DV:(hh + 1) * GLA_DV],
                                     preferred_element_type=F32))
            qin = q0 + q1
            st = st_scr[...]
            inter = [None] * per
            for i in (range(per - 1, -1, -1) if reverse else range(per)):
                sl = slice(i * c, (i + 1) * c)
                inter[i] = lax.dot_general(qin[sl], st.astype(BF16), NT_DIMS,
                                           preferred_element_type=F32)
                kv_t = lax.dot_general(vb[sl], kst[sl], TN_DIMS, preferred_element_type=F32)
                st = st * dec[i * c:i * c + 1, :] + kv_t * same_head
            st_scr[...] = st
            o = jnp.concatenate(intra, axis=1) + jnp.concatenate(inter, axis=0)
            if not reverse:
                o_scr[rows, :] = o
            else:
                o = o + o_scr[rows, :]
                normed = []
                for hh in range(2):
                    oh = o[:, hh * GLA_DV:(hh + 1) * GLA_DV]
                    normed.append(oh * lax.rsqrt(jnp.mean(oh * oh, axis=-1, keepdims=True) + EPS))
                g = g_ref[0, rows, :]
                o_ref[0, rows, :] = (g * _sigmoid(g)
                                     * jnp.concatenate(normed, axis=1)).astype(o_ref.dtype)
            return carry

        lax.fori_loop(0, nblocks, recur, 0, unroll=4)

    direction(wf_ref, bf_ref, False)
    direction(wb_ref, bb_ref, True)


def _gla(proj, w_gate, b_gate, col0):
    bsz, seq_len, _ = proj.shape
    pairs = GLA_HEADS // 2
    dk2, dv2 = 2 * GLA_DK, 2 * GLA_DV
    c_q = col0
    c_k = c_q + GLA_HEADS * GLA_DK
    c_v = c_k + GLA_HEADS * GLA_DK
    c_g = c_v + GLA_HEADS * GLA_DV
    c_lr = c_g + GLA_HEADS * GLA_DV
    wg = w_gate.astype(F32).reshape(2, GLA_RANK, pairs, dk2).transpose(0, 2, 1, 3)
    wf = jnp.zeros((pairs, LANES, dk2), F32).at[:, :GLA_RANK].set(wg[0])
    wb = jnp.zeros((pairs, LANES, dk2), F32).at[:, GLA_RANK:2 * GLA_RANK].set(wg[1])
    bg = b_gate.astype(F32).reshape(2, pairs, 1, dk2)
    tok = lambda col, width: (lambda b, p: (b, 0, col // width + p))
    mat = lambda b, p: (p, 0, 0)
    return pl.pallas_call(
        functools.partial(_gla_kernel, nblocks=seq_len // GLA_BLOCK),
        grid=(bsz, pairs),
        in_specs=[
            pl.BlockSpec((1, seq_len, dk2), tok(c_q, dk2)),
            pl.BlockSpec((1, seq_len, dk2), tok(c_k, dk2)),
            pl.BlockSpec((1, seq_len, dv2), tok(c_v, dv2)),
            pl.BlockSpec((1, seq_len, dv2), tok(c_g, dv2)),
            pl.BlockSpec((1, seq_len, LANES), lambda b, p: (b, 0, c_lr // LANES)),
            pl.BlockSpec((1, LANES, dk2), mat),
            pl.BlockSpec((1, LANES, dk2), mat),
            pl.BlockSpec((1, 1, dk2), mat),
            pl.BlockSpec((1, 1, dk2), mat),
        ],
        out_specs=pl.BlockSpec((1, seq_len, dv2), lambda b, p: (b, 0, p)),
        out_shape=jax.ShapeDtypeStruct((bsz, seq_len, GLA_HEADS * GLA_DV), BF16),
        scratch_shapes=(
            [pltpu.VMEM((seq_len, dk2), BF16)] * 4
            + [pltpu.VMEM((seq_len, dk2), F32), pltpu.VMEM((seq_len, dv2), F32),
               pltpu.VMEM((dv2, dk2), F32)]
        ),
        compiler_params=_params("parallel", "parallel"),
        name="gla",
    )(proj, proj, proj, proj, proj, wf, wb, bg[0], bg[1])


def _swap(x):
    return pltpu.roll(x, S5_STATE, 1)


def _s5_prep_kernel(lre_ref, lim_ref, ldt_ref, brr_ref, bis_ref, cr_ref, ci_ref,
                    m_ref, e_ref, ft_ref, a_ref):
    t_len = S5_CHUNK
    gsz = S5_GROUP
    width = gsz * t_len
    first = lax.broadcasted_iota(jnp.int32, (1, LANES), 1) < S5_STATE
    one_zero = jnp.where(first, 1.0, 0.0).astype(F32)
    sign = jnp.where(first, 1.0, -1.0).astype(F32)

    def packed_exp(t, are, aim):
        ph = t * aim
        return jnp.exp(t * are) * jnp.where(first, jnp.cos(ph), jnp.sin(ph))

    def dup_re(x):
        return jnp.where(first, x, _swap(x))

    def dup_im_signed(x):
        return jnp.where(first, -_swap(x), x)

    def per_channel(pw, w_a, w_b):
        return jnp.concatenate([pw * w_a[c:c + 1, :] + _swap(pw) * w_b[c:c + 1, :]
                                for c in range(gsz)], axis=0)

    trow = lax.broadcasted_iota(jnp.int32, (t_len, 1), 0).astype(F32)
    panels = []
    for d in range(2):
        lre = lre_ref[0, d:d + 1, :]
        lim = lim_ref[0, d:d + 1, :]
        dt = jnp.exp(ldt_ref[0, d:d + 1, :])
        are, aim = lre * dt, lim * dt
        num = packed_exp(jnp.ones((1, 1), F32), are, aim) - one_zero
        inv_den = 1.0 / (lre * lre + lim * lim)
        coef = num * (lre * inv_den) + _swap(num) * (lim * inv_den * sign)
        bbar = coef * brr_ref[0, d] + _swap(coef) * bis_ref[0, d]
        if d == 0:
            t_e, t_f, t_g = t_len - 1.0 - trow, trow + 1.0, trow
        else:
            t_e, t_f, t_g = trow, t_len - trow, t_len - 1.0 - trow
        e_mat = per_channel(packed_exp(t_e, are, aim), dup_re(bbar), dup_im_signed(bbar))
        e_ref[0, :, d * LANES:(d + 1) * LANES] = e_mat.astype(BF16)
        cr, ci = cr_ref[0, d], ci_ref[0, d]
        ft_ref[0, d] = per_channel(packed_exp(t_f, are, aim), cr, ci).astype(BF16)
        g_mat = per_channel(packed_exp(t_g, are, aim), cr, ci)
        panels.append(lax.dot_general(bbar, g_mat, NT_DIMS, preferred_element_type=F32,
                                      precision=HIGHEST))
        kcol = lax.broadcasted_iota(jnp.int32, (8, 1), 0)
        tk = (t_len * jnp.left_shift(1, kcol)).astype(F32)
        ak = packed_exp(tk, are, aim)
        a_ref[0, d, 0:8, :] = dup_re(ak)
        a_ref[0, d, 8:16, :] = dup_im_signed(ak)

    p_fwd, p_bwd = panels
    j = lax.broadcasted_iota(jnp.int32, (t_len, width), 0)
    pos = lax.broadcasted_iota(jnp.int32, (t_len, width), 1) % t_len
    for c in range(gsz):
        fwd = pltpu.roll(jnp.broadcast_to(p_fwd[c:c + 1, :], (t_len, width)), 0, 1,
                         stride=1, stride_axis=0)
        bwd = pltpu.roll(jnp.broadcast_to(p_bwd[c:c + 1, :], (t_len, width)), width - (t_len - 1), 1,
                         stride=1, stride_axis=0)
        blk = jnp.where(pos >= j, fwd, 0.0) + jnp.where(pos <= j, bwd, 0.0)
        m_ref[0, c * t_len:(c + 1) * t_len, :] = blk.astype(BF16)


def _s5_apply_kernel(u_ref, m_ref, e_ref, ft_ref, a_ref, d_ref, y_ref, *, blocks_per_seq, nsteps):
    t_len = S5_CHUNK
    per = LANES // t_len
    nblk = u_ref.shape[1]
    xc = [u_ref[c] for c in range(S5_GROUP)]
    ys, sums = [], []
    for q in range(per):
        u = jnp.concatenate([x[:, q * t_len:(q + 1) * t_len] for x in xc], axis=1)
        ub = u.astype(BF16)
        ys.append(jnp.dot(ub, m_ref[0], preferred_element_type=F32) + u * d_ref[0])
        sums.append(jnp.dot(ub, e_ref[0], preferred_element_type=F32))
    nidx = lax.broadcasted_iota(jnp.int32, (nblk, LANES), 0) % blocks_per_seq
    for d in range(2):

        def block_shift(val, sh):
            if d == 0:
                return jnp.where(nidx >= sh, pltpu.roll(val, sh, 0), 0.0)
            return jnp.where(nidx < blocks_per_seq - sh, pltpu.roll(val, nblk - sh, 0), 0.0)

        def chunk_shift(x, sh):
            if sh % per == 0:
                return [block_shift(v, sh // per) for v in x]
            src = [q - sh if d == 0 else q + sh for q in range(per)]
            return [x[s] if 0 <= s < per else block_shift(x[s % per], 1) for s in src]

        x = [s[:, d * LANES:(d + 1) * LANES] for s in sums]
        for k in range(nsteps):
            xs = chunk_shift(x, 1 << k)
            x = [v + a_ref[0, d, k:k + 1, :] * w + a_ref[0, d, 8 + k:9 + k, :] * _swap(w)
                 for v, w in zip(x, xs)]
        for q, v in enumerate(chunk_shift(x, 1)):
            ys[q] = ys[q] + lax.dot_general(v.astype(BF16), ft_ref[0, d], NT_DIMS,
                                            preferred_element_type=F32)
    for c in range(S5_GROUP):
        y_ref[c] = jnp.concatenate([y[:, c * t_len:(c + 1) * t_len] for y in ys], axis=1)


def _s5(su_t, seq_len, lam_re, lam_im, log_dt, b_re, b_im, c_re, c_im, d_skip):
    w, nblk, _ = su_t.shape
    g = w // S5_GROUP
    t_len = S5_CHUNK
    nchunks = seq_len // t_len
    nsteps = max(1, (nchunks - 1).bit_length())
    assert nsteps <= 8 and seq_len % LANES == 0
    width = S5_GROUP * t_len
    dup = lambda a: jnp.concatenate([a, a], axis=-1)
    gd = lambda a: jnp.moveaxis(a.astype(F32), 0, 1)
    lre = dup(gd(lam_re))
    lim = dup(gd(lam_im))
    ldt = jnp.broadcast_to(gd(log_dt)[..., None], (g, 2, LANES))
    bt_re = jnp.swapaxes(gd(b_re), -1, -2)
    bt_im = jnp.swapaxes(gd(b_im), -1, -2)
    brr = jnp.concatenate([bt_re, bt_re], axis=-1)
    bis = jnp.concatenate([-bt_im, bt_im], axis=-1)
    cre, cim = gd(c_re), gd(c_im)
    cr = jnp.concatenate([cre, -cre], axis=-1)
    ci = jnp.concatenate([-cim, -cim], axis=-1)
    vec3 = pl.BlockSpec((1, 2, LANES), lambda i: (i, 0, 0))
    mat4 = pl.BlockSpec((1, 2, S5_GROUP, LANES), lambda i: (i, 0, 0, 0))
    m_mat, e_mat, ft_mat, a_mat = pl.pallas_call(
        _s5_prep_kernel,
        grid=(g,),
        in_specs=[vec3, vec3, vec3, mat4, mat4, mat4, mat4],
        out_specs=[
            pl.BlockSpec((1, width, width), lambda i: (i, 0, 0)),
            pl.BlockSpec((1, width, 2 * LANES), lambda i: (i, 0, 0)),
            pl.BlockSpec((1, 2, width, LANES), lambda i: (i, 0, 0, 0)),
            pl.BlockSpec((1, 2, 16, LANES), lambda i: (i, 0, 0, 0)),
        ],
        out_shape=[
            jax.ShapeDtypeStruct((g, width, width), BF16),
            jax.ShapeDtypeStruct((g, width, 2 * LANES), BF16),
            jax.ShapeDtypeStruct((g, 2, width, LANES), BF16),
            jax.ShapeDtypeStruct((g, 2, 16, LANES), F32),
        ],
        compiler_params=_params("parallel"),
        name="s5_prep",
    )(lre, lim, ldt, brr, bis, cr, ci)
    d_rep = jnp.repeat(d_skip.astype(F32).reshape(g, 1, S5_GROUP), t_len, axis=-1)
    blk = pl.BlockSpec((S5_GROUP, nblk, LANES), lambda i: (i, 0, 0))
    return pl.pallas_call(
        functools.partial(_s5_apply_kernel, blocks_per_seq=seq_len // LANES, nsteps=nsteps),
        grid=(g,),
        in_specs=[
            blk,
            pl.BlockSpec((1, width, width), lambda i: (i, 0, 0)),
            pl.BlockSpec((1, width, 2 * LANES), lambda i: (i, 0, 0)),
            pl.BlockSpec((1, 2, width, LANES), lambda i: (i, 0, 0, 0)),
            pl.BlockSpec((1, 2, 16, LANES), lambda i: (i, 0, 0, 0)),
            pl.BlockSpec((1, 1, width), lambda i: (i, 0, 0)),
        ],
        out_specs=blk,
        out_shape=jax.ShapeDtypeStruct(su_t.shape, F32),
        compiler_params=_params("parallel"),
        name="s5_apply",
    )(su_t, m_mat, e_mat, ft_mat, a_mat, d_rep)


def _outproj_kernel(h_ref, gt_ref, yr_ref, ys_ref, yg_ref, wglu_ref, bglu_ref, wrg_ref, ws_ref, o_ref,
                    *, steps):
    sub = o_ref.shape[0] // LANES
    base = (pl.program_id(0) % steps) * sub
    y = jnp.concatenate([ys_ref[:, pl.ds(base + k, 1), :][:, 0, :] for k in range(sub)], axis=1)
    z = y * (0.5 * (1.0 + jnp.tanh(math.sqrt(2.0 / math.pi) * (y + 0.044715 * (y * y * y)))))
    gl = jnp.dot(wglu_ref[...], z.astype(BF16), preferred_element_type=F32) + bglu_ref[...]
    s5 = (z * _sigmoid(gl)).astype(BF16)
    acc = jnp.dot(jnp.concatenate([yr_ref[...], yg_ref[...]], axis=1), wrg_ref[...],
                  preferred_element_type=F32)
    acc = acc + lax.dot_general(s5, ws_ref[...], TN_DIMS, preferred_element_type=F32)
    o_ref[...] = h_ref[...] + gt_ref[0] * acc


def _outproj(h, seq_len, gate, y_ret, y_s5_t, y_gla, wglu_t, b_glu, w_rg, w_s, layer, *, tm=512):
    m, d = h.shape
    tm = min(tm, seq_len)
    per_batch = seq_len // tm
    steps = SUBLANES * LANES // tm
    w_s5 = y_s5_t.shape[0]
    row = lambda width: pl.BlockSpec((tm, width), lambda i: (i, 0))
    whole = lambda a: pl.BlockSpec((None,) + a.shape[1:], lambda i: (layer, 0, 0),
                                   pipeline_mode=pl.Buffered(1))
    bglu = b_glu.astype(F32).reshape(-1, w_s5, 1)
    return pl.pallas_call(
        functools.partial(_outproj_kernel, steps=steps),
        grid=(m // tm,),
        in_specs=[
            row(d),
            pl.BlockSpec((1, 1, d), lambda i: (i // per_batch, 0, 0)),
            row(y_ret.shape[1]),
            pl.BlockSpec((w_s5, SUBLANES, LANES), lambda i: (0, i // steps, 0)),
            row(y_gla.shape[1]),
            whole(wglu_t), whole(bglu), whole(w_rg), whole(w_s),
        ],
        out_specs=row(d),
        out_shape=jax.ShapeDtypeStruct((m, d), F32),
        compiler_params=_params("parallel"),
        name="outproj",
    )(h, gate, y_ret, y_s5_t, y_gla, wglu_t, bglu, w_rg, w_s)


def kernel(x, c, w_ada, b_ada, g_ffn1, ffn1_w1, ffn1_w3, ffn1_w2, g_mix, w_in, s5_lam_re, s5_lam_im, s5_log_dt, s5_b_re, s5_b_im, s5_c_re, s5_c_im, s5_d, s5_w_glu, s5_b_glu, gla_w_gate, gla_b_gate, w_out, g_ffn2, ffn2_w1, ffn2_w3, ffn2_w2, g_final):
    bsz, seq_len, d = x.shape
    depth = w_ada.shape[0]
    mod = _ada_mod(c, w_ada, b_ada).reshape(depth, bsz, N_MOD, 1, d)
    tables = _ret_tables(seq_len)
    ret_w = RET_HEADS * (2 * RET_DK + 2 * RET_DV)
    s5_w = s5_d.shape[1]
    h = x.reshape(bsz * seq_len, d)
    ffn1_w1, ffn1_w3, ffn1_w2, ffn2_w1, ffn2_w3, ffn2_w2 = (
        w.astype(BF16) for w in (ffn1_w1, ffn1_w3, ffn1_w2, ffn2_w1, ffn2_w3, ffn2_w2))
    w_in = w_in.astype(BF16)
    w_tok = jnp.concatenate([w_in[:, :, :ret_w], w_in[:, :, ret_w + s5_w:]], axis=2)
    w_tok = jnp.pad(w_tok, ((0, 0), (0, 0), (0, -w_tok.shape[2] % LANES)))
    w_s5_t = jnp.swapaxes(w_in[:, :, ret_w:ret_w + s5_w], 1, 2)
    w_out = w_out.astype(BF16)
    ret_o = RET_HEADS * RET_DV
    w_rg = jnp.concatenate([w_out[:, :ret_o], w_out[:, ret_o + s5_w:]], axis=1)
    w_s = w_out[:, ret_o:ret_o + s5_w]
    w_glu_t = jnp.swapaxes(s5_w_glu.astype(BF16), 1, 2)
    for l in range(depth):
        sh1, sc1, gt1, sh2, sc2, gt2, sh3, sc3, gt3 = (mod[l, :, i] for i in range(N_MOD))
        h = _ffn(h, seq_len, g_ffn1[l], sh1, sc1, gt1, ffn1_w1, ffn1_w3, ffn1_w2, l)
        proj, su_t = _inproj(h, seq_len, g_mix[l], sh2, sc2, w_tok, w_s5_t, l)
        proj = proj.reshape(bsz, seq_len, -1)
        y_ret = _retention(proj, tables)
        y_s5_t = _s5(su_t, seq_len, s5_lam_re[l], s5_lam_im[l], s5_log_dt[l],
                     s5_b_re[l], s5_b_im[l], s5_c_re[l], s5_c_im[l], s5_d[l])
        y_gla = _gla(proj, gla_w_gate[l], gla_b_gate[l], ret_w)
        h = _outproj(h, seq_len, gt2, y_ret.reshape(bsz * seq_len, -1), y_s5_t,
                     y_gla.reshape(bsz * seq_len, -1), w_glu_t, s5_b_glu, w_rg, w_s, l)
        h = _ffn(h, seq_len, g_ffn2[l], sh3, sc3, gt3, ffn2_w1, ffn2_w3, ffn2_w2, l,
                 g_final if l == depth - 1 else None)
    return h.reshape(bsz, seq_len, d)
```

```python
import functools
import math

import jax
import jax.numpy as jnp
from jax import lax
from jax.experimental import pallas as pl
from jax.experimental.pallas import tpu as pltpu

F32 = jnp.float32
BF16 = jnp.bfloat16
HIGHEST = lax.Precision.HIGHEST

RET_HEADS = 4
RET_DK = 128
RET_DV = 256
RET_CHUNK = 256
ROPE_BASE = 10000.0
S5_GROUP = 16
S5_STATE = 64
S5_CHUNK = 32
GLA_HEADS = 4
GLA_DK = 64
GLA_DV = 128
GLA_RANK = 16
GLA_TAU = 16.0
GLA_CHUNK = 64
GLA_BLOCK = 256
FFN_RES = 0.5
FFN_ROWS = 512
N_MOD = 9
EPS = 1e-6

LANES = 128
SUBLANES = 8
MXU_N = 256
VMEM_LIMIT = 60 << 20

NT_DIMS = (((1,), (1,)), ((), ()))
TN_DIMS = (((0,), (0,)), ((), ()))


def _params(*semantics):
    return pltpu.CompilerParams(dimension_semantics=semantics, vmem_limit_bytes=VMEM_LIMIT)


def _sigmoid(x):
    return 1.0 / (1.0 + jnp.exp(-x))


def _norm_mod(x, g, shift, scale):
    ms = jnp.mean(x * x, axis=-1, keepdims=True)
    return x * lax.rsqrt(ms + EPS) * (g * (1.0 + scale)) + shift


def _ada_kernel(c_ref, w_ref, b_ref, o_ref):
    c = c_ref[...]
    cond = c * _sigmoid(c)
    rows = cond.shape[0]
    c_hi = cond.astype(BF16).astype(F32)
    lhs = jnp.concatenate([c_hi, cond - c_hi], axis=0).astype(BF16)
    w = w_ref[0]
    w_hi = w.astype(BF16)
    w_mid = (w - w_hi.astype(F32)).astype(BF16)
    s = (jnp.dot(lhs, w_hi, preferred_element_type=F32)
         + jnp.dot(lhs, w_mid, preferred_element_type=F32))
    o_ref[0] = s[:rows] + s[rows:] + b_ref[0]


def _ada_mod(c, w_ada, b_ada):
    depth, d, n = w_ada.shape
    bsz = c.shape[0]
    rows = -(-bsz // 8) * 8
    c_pad = jnp.zeros((rows, d), F32).at[:bsz].set(c)
    tn = n // 16
    out = pl.pallas_call(
        _ada_kernel,
        grid=(depth, n // tn),
        in_specs=[
            pl.BlockSpec((rows, d), lambda l, j: (0, 0)),
            pl.BlockSpec((1, d, tn), lambda l, j: (l, 0, j)),
            pl.BlockSpec((1, 1, tn), lambda l, j: (l, 0, j)),
        ],
        out_specs=pl.BlockSpec((1, rows, tn), lambda l, j: (l, 0, j)),
        out_shape=jax.ShapeDtypeStruct((depth, rows, n), F32),
        compiler_params=_params("parallel", "parallel"),
        name="ada_mod",
    )(c_pad, w_ada, b_ada.reshape(depth, 1, n))
    return out[:, :bsz]


def _ffn_kernel(h_ref, g_ref, sh_ref, sc_ref, gt_ref, w1_ref, w3_ref, w2_ref, *rest, tf, nj, final):
    if final:
        gf_ref, o_ref, u_scr = rest
    else:
        o_ref, u_scr = rest
    j = pl.program_id(1)

    row_blocks = [slice(r0, r0 + FFN_ROWS) for r0 in range(0, o_ref.shape[0], FFN_ROWS)]

    def normalize(rows):
        u_scr[rows, :] = _norm_mod(h_ref[rows, :], g_ref[...], sh_ref[0], sc_ref[0]).astype(BF16)

    def matmuls(rows, first):
        u = u_scr[rows, :]
        acts = []
        for c0 in range(0, tf, MXU_N):
            a = jnp.dot(u, w1_ref[:, c0:c0 + MXU_N], preferred_element_type=F32)
            b = jnp.dot(u, w3_ref[:, c0:c0 + MXU_N], preferred_element_type=F32)
            acts.append((a * _sigmoid(a) * b).astype(BF16))
        down = jnp.dot(jnp.concatenate(acts, axis=1), w2_ref[...], preferred_element_type=F32)
        o_ref[rows, :] = down if first else o_ref[rows, :] + down

    def finish(rows):
        out = h_ref[rows, :] + (FFN_RES * gt_ref[0]) * o_ref[rows, :]
        if final:
            ms = jnp.mean(out * out, axis=-1, keepdims=True)
            out = out * lax.rsqrt(ms + EPS) * gf_ref[...]
        o_ref[rows, :] = out

    @pl.when(j == 0)
    def _():
        for rows in row_blocks:
            normalize(rows)
            matmuls(rows, True)

    @pl.when(jnp.logical_and(j > 0, j < nj - 1))
    def _():
        for rows in row_blocks:
            matmuls(rows, False)

    @pl.when(j == nj - 1)
    def _():
        for rows in row_blocks:
            matmuls(rows, False)
            finish(rows)


def _ffn(h, seq_len, g, shift, scale, gate, w1, w3, w2, layer, g_final=None, *, tm=1024, tf=512):
    m, d = h.shape
    dff = w1.shape[2]
    tm = min(tm, seq_len)
    nj = dff // tf
    assert nj >= 2 and tm % FFN_ROWS == 0
    per_batch = seq_len // tm
    final = g_final is not None
    vec = lambda i, j: (i // per_batch, 0, 0)
    in_specs = [
        pl.BlockSpec((tm, d), lambda i, j: (i, 0)),
        pl.BlockSpec((1, d), lambda i, j: (0, 0)),
        pl.BlockSpec((1, 1, d), vec),
        pl.BlockSpec((1, 1, d), vec),
        pl.BlockSpec((1, 1, d), vec),
        pl.BlockSpec((None, d, tf), lambda i, j: (layer, 0, j)),
        pl.BlockSpec((None, d, tf), lambda i, j: (layer, 0, j)),
        pl.BlockSpec((None, tf, d), lambda i, j: (layer, j, 0)),
    ]
    args = [h, g.reshape(1, d), shift, scale, gate, w1, w3, w2]
    if final:
        in_specs.append(pl.BlockSpec((1, d), lambda i, j: (0, 0)))
        args.append(g_final.reshape(1, d))
    return pl.pallas_call(
        functools.partial(_ffn_kernel, tf=tf, nj=nj, final=final),
        grid=(m // tm, nj),
        in_specs=in_specs,
        out_specs=pl.BlockSpec((tm, d), lambda i, j: (i, 0)),
        out_shape=jax.ShapeDtypeStruct((m, d), F32),
        scratch_shapes=[pltpu.VMEM((tm, d), BF16)],
        compiler_params=_params("parallel", "arbitrary"),
        name="ffn",
    )(*args)


def _inproj_kernel(h_ref, g_ref, sh_ref, sc_ref, w_ref, wst_ref, o_ref, ost_ref, *, steps):
    u = _norm_mod(h_ref[...], g_ref[...], sh_ref[0], sc_ref[0]).astype(BF16)
    o_ref[...] = jnp.dot(u, w_ref[...], preferred_element_type=F32)
    su_t = lax.dot_general(wst_ref[...], u, NT_DIMS, preferred_element_type=F32)
    sub = su_t.shape[1] // LANES
    base = (pl.program_id(0) % steps) * sub
    for k in range(sub):
        ost_ref[:, pl.ds(base + k, 1), :] = su_t[:, k * LANES:(k + 1) * LANES][:, None, :]


def _inproj(h, seq_len, g, shift, scale, w, w_s5_t, layer, *, tm=256):
    m, d = h.shape
    d_pad = w.shape[2]
    w_s5 = w_s5_t.shape[1]
    tm = min(tm, seq_len)
    per_batch = seq_len // tm
    steps = SUBLANES * LANES // tm
    vec = lambda i: (i // per_batch, 0, 0)
    return pl.pallas_call(
        functools.partial(_inproj_kernel, steps=steps),
        grid=(m // tm,),
        in_specs=[
            pl.BlockSpec((tm, d), lambda i: (i, 0)),
            pl.BlockSpec((1, d), lambda i: (0, 0)),
            pl.BlockSpec((1, 1, d), vec),
            pl.BlockSpec((1, 1, d), vec),
            pl.BlockSpec((None, d, d_pad), lambda i: (layer, 0, 0), pipeline_mode=pl.Buffered(1)),
            pl.BlockSpec((None, w_s5, d), lambda i: (layer, 0, 0), pipeline_mode=pl.Buffered(1)),
        ],
        out_specs=[pl.BlockSpec((tm, d_pad), lambda i: (i, 0)),
                   pl.BlockSpec((w_s5, SUBLANES, LANES), lambda i: (0, i // steps, 0))],
        out_shape=[jax.ShapeDtypeStruct((m, d_pad), F32),
                   jax.ShapeDtypeStruct((w_s5, m // LANES, LANES), F32)],
        compiler_params=_params("arbitrary"),
        name="inproj",
    )(h, g.reshape(1, d), shift, scale, w, w_s5_t)


def _ret_tables(seq_len):
    c = RET_CHUNK
    pos = jnp.arange(seq_len, dtype=F32)
    inv_freq = ROPE_BASE ** (-jnp.arange(0, RET_DK, 2, dtype=F32) / RET_DK)
    ang = pos[:, None] * inv_freq[None, :]
    cos, sin = jnp.cos(ang), jnp.sin(ang)
    cs = jnp.concatenate([cos, cos], axis=-1)
    sn = jnp.concatenate([-sin, sin], axis=-1)
    lg = jnp.log1p(-jnp.exp2(-5.0 - jnp.arange(RET_HEADS, dtype=F32)))
    idx = jnp.arange(c, dtype=F32)
    dist = jnp.abs(idx[:, None] - idx[None, :])
    dmat = jnp.exp(dist[None] * lg[:, None, None])
    rows = jnp.stack([c - 1.0 - idx, idx, idx + 1.0, c - idx])
    dvec = jnp.exp(rows[None] * lg[:, None, None])
    dvec = jnp.broadcast_to(dvec[..., None], dvec.shape + (RET_DK,))
    gc = jnp.broadcast_to(jnp.exp(c * lg)[:, None, None], (RET_HEADS, 1, RET_DV))
    return cs, sn, dmat, dvec, gc


def _ret_kernel(q_ref, k_ref, v_ref, g_ref, cs_ref, sn_ref, dmat_ref, dvec_ref, gc_ref, o_ref,
                q_scr, k_scr, qf_scr, qb_scr, kf_scr, kb_scr, of_scr, ob_scr, sf_scr, sb_scr,
                *, nchunks):
    c = RET_CHUNK
    scale = RET_DK ** -0.5
    chunk_rows = lambda n: pl.ds(pl.multiple_of(n * c, c), c)

    def prepare(n, carry):
        rows = chunk_rows(n)
        cs = cs_ref[rows, :]
        sn = sn_ref[rows, :]
        q = q_ref[0, rows, :]
        k = k_ref[0, rows, :]
        q = (q * cs + pltpu.roll(q, RET_DK // 2, 1) * sn) * scale
        k = k * cs + pltpu.roll(k, RET_DK // 2, 1) * sn
        q_scr[rows, :] = q.astype(BF16)
        k_scr[rows, :] = k.astype(BF16)
        kf_scr[rows, :] = (k * dvec_ref[0, 0]).astype(BF16)
        kb_scr[rows, :] = (k * dvec_ref[0, 1]).astype(BF16)
        qf_scr[rows, :] = (q * dvec_ref[0, 2]).astype(BF16)
        qb_scr[rows, :] = (q * dvec_ref[0, 3]).astype(BF16)
        return carry

    lax.fori_loop(0, nchunks, prepare, 0, unroll=4)
    sf_scr[...] = jnp.zeros_like(sf_scr)
    sb_scr[...] = jnp.zeros_like(sb_scr)
    gc = gc_ref[0]

    def recur(n, carry):
        rf = chunk_rows(n)
        vf = v_ref[0, rf, :].astype(BF16)
        scores = lax.dot_general(q_scr[rf, :], k_scr[rf, :], NT_DIMS,
                                 preferred_element_type=F32) * dmat_ref[0]
        s = sf_scr[...]
        of_scr[rf, :] = (jnp.dot(scores.astype(BF16), vf, preferred_element_type=F32)
                         + jnp.dot(qf_scr[rf, :], s.astype(BF16), preferred_element_type=F32))
        sf_scr[...] = gc * s + lax.dot_general(kf_scr[rf, :], vf, TN_DIMS,
                                               preferred_element_type=F32)
        rb = chunk_rows(nchunks - 1 - n)
        vb = v_ref[0, rb, :].astype(BF16)
        s = sb_scr[...]
        ob_scr[rb, :] = jnp.dot(qb_scr[rb, :], s.astype(BF16), preferred_element_type=F32)
        sb_scr[...] = gc * s + lax.dot_general(kb_scr[rb, :], vb, TN_DIMS,
                                               preferred_element_type=F32)
        return carry

    lax.fori_loop(0, nchunks, recur, 0, unroll=4)

    def finish(n, carry):
        rows = chunk_rows(n)
        o = of_scr[rows, :] + ob_scr[rows, :]
        oc = o - jnp.mean(o, axis=-1, keepdims=True)
        ln = oc * lax.rsqrt(jnp.mean(oc * oc, axis=-1, keepdims=True) + EPS)
        g = g_ref[0, rows, :]
        o_ref[0, rows, :] = (g * _sigmoid(g) * ln).astype(o_ref.dtype)
        return carry

    lax.fori_loop(0, nchunks, finish, 0, unroll=4)


def _retention(proj, tables):
    bsz, seq_len, _ = proj.shape
    cs, sn, dmat, dvec, gc = tables
    h = RET_HEADS
    kq, kk = 0, h
    kv, kg = (2 * h * RET_DK) // RET_DV, (2 * h * RET_DK + h * RET_DV) // RET_DV
    tok = lambda off: (lambda b, i: (b, 0, off + i))
    return pl.pallas_call(
        functools.partial(_ret_kernel, nchunks=seq_len // RET_CHUNK),
        grid=(bsz, h),
        in_specs=[
            pl.BlockSpec((1, seq_len, RET_DK), tok(kq)),
            pl.BlockSpec((1, seq_len, RET_DK), tok(kk)),
            pl.BlockSpec((1, seq_len, RET_DV), tok(kv)),
            pl.BlockSpec((1, seq_len, RET_DV), tok(kg)),
            pl.BlockSpec((seq_len, RET_DK), lambda b, i: (0, 0), pipeline_mode=pl.Buffered(1)),
            pl.BlockSpec((seq_len, RET_DK), lambda b, i: (0, 0), pipeline_mode=pl.Buffered(1)),
            pl.BlockSpec((1, RET_CHUNK, RET_CHUNK), lambda b, i: (i, 0, 0)),
            pl.BlockSpec((1, 4, RET_CHUNK, RET_DK), lambda b, i: (i, 0, 0, 0)),
            pl.BlockSpec((1, 1, RET_DV), lambda b, i: (i, 0, 0)),
        ],
        out_specs=pl.BlockSpec((1, seq_len, RET_DV), lambda b, i: (b, 0, i)),
        out_shape=jax.ShapeDtypeStruct((bsz, seq_len, h * RET_DV), BF16),
        scratch_shapes=(
            [pltpu.VMEM((seq_len, RET_DK), BF16)] * 6
            + [pltpu.VMEM((seq_len, RET_DV), F32)] * 2
            + [pltpu.VMEM((RET_DK, RET_DV), F32)] * 2
        ),
        compiler_params=_params("parallel", "parallel"),
        name="retention",
    )(proj, proj, proj, proj, cs, sn, dmat, dvec, gc)


def _gla_kernel(q_ref, k_ref, v_ref, g_ref, lr_ref, wf_ref, wb_ref, bf_ref, bb_ref, o_ref,
                q0_scr, q1_scr, kin_scr, kst_scr, dec_scr, o_scr, st_scr, *, nblocks):
    c, r = GLA_CHUNK, GLA_BLOCK
    per = r // c
    dk2, dv2 = 2 * GLA_DK, 2 * GLA_DV
    scale = GLA_DK ** -0.5
    block_rows = lambda m: pl.ds(pl.multiple_of(m * r, r), r)

    ri = lax.broadcasted_iota(jnp.int32, (r, r), 0)
    ci = lax.broadcasted_iota(jnp.int32, (r, r), 1)
    same_chunk = (ri // c) == (ci // c)
    lane = lax.broadcasted_iota(jnp.int32, (r, dk2), 1)
    se = lax.broadcasted_iota(jnp.int32, (dv2, dk2), 0)
    sd = lax.broadcasted_iota(jnp.int32, (dv2, dk2), 1)
    same_head = jnp.where(se < GLA_DV, jnp.where(sd < GLA_DK, 1.0, 0.0),
                          jnp.where(sd >= GLA_DK, 1.0, 0.0)).astype(F32)

    def direction(w_ref, b_ref, reverse):
        if reverse:
            cum = same_chunk & (ci >= ri)
            keep = same_chunk & (ci > ri)
        else:
            cum = same_chunk & (ci <= ri)
            keep = same_chunk & (ci <= ri)
        sum_mat = jnp.concatenate([jnp.where(cum, 1.0, 0.0), jnp.where(same_chunk, 1.0, 0.0)],
                                  axis=0).astype(BF16)

        def prepare(m, carry):
            rows = block_rows(m)
            lr = lr_ref[0, rows, :]
            lr_hi = lr.astype(BF16)
            lr_mid = (lr - lr_hi.astype(F32)).astype(BF16)
            x = jnp.dot(jnp.concatenate([lr_hi, lr_mid, lr_hi], axis=1), w_ref[0],
                        preferred_element_type=F32) + b_ref[0]
            la = (jnp.minimum(x, 0.0) - jnp.log(1.0 + jnp.exp(-jnp.abs(x)))) * (1.0 / GLA_TAU)
            hi = la.astype(BF16)
            rest = la - hi.astype(F32)
            mid = rest.astype(BF16)
            lo = (rest - mid.astype(F32)).astype(BF16)
            sums = jnp.dot(sum_mat, jnp.concatenate([hi, mid, lo], axis=1),
                           preferred_element_type=F32)
            sums = sums[:, :dk2] + sums[:, dk2:2 * dk2] + sums[:, 2 * dk2:]
            b, total = sums[:r], sums[r:]
            q_in = q_ref[0, rows, :] * scale * jnp.exp(b)
            k = k_ref[0, rows, :]
            q0_scr[rows, :] = jnp.where(lane < GLA_DK, q_in, 0.0).astype(BF16)
            q1_scr[rows, :] = jnp.where(lane >= GLA_DK, q_in, 0.0).astype(BF16)
            kin_scr[rows, :] = (k * jnp.exp(-b)).astype(BF16)
            kst_scr[rows, :] = (k * jnp.exp(total - b)).astype(BF16)
            dec_scr[rows, :] = jnp.exp(total)
            return carry

        lax.fori_loop(0, nblocks, prepare, 0, unroll=4)
        st_scr[...] = jnp.zeros_like(st_scr)

        def recur(t, carry):
            rows = block_rows(nblocks - 1 - t if reverse else t)
            q0 = q0_scr[rows, :]
            q1 = q1_scr[rows, :]
            kin = kin_scr[rows, :]
            kst = kst_scr[rows, :]
            dec = dec_scr[rows, :]
            vb = v_ref[0, rows, :].astype(BF16)
            intra = []
            for hh, qh in enumerate((q0, q1)):
                scores = lax.dot_general(qh, kin, NT_DIMS, preferred_element_type=F32)
                scores = jnp.where(keep, scores, 0.0).astype(BF16)
                intra.append(jnp.dot(scores, vb[:, hh * GLA_DV:(hh + 1) * GLA_DV],
                                     preferred_element_type=F32))
            qin = q0 + q1
            st = st_scr[...]
            inter = [None] * per
            for i in (range(per - 1, -1, -1) if reverse else range(per)):
                sl = slice(i * c, (i + 1) * c)
                inter[i] = lax.dot_general(qin[sl], st.astype(BF16), NT_DIMS,
                                           preferred_element_type=F32)
                kv_t = lax.dot_general(vb[sl], kst[sl], TN_DIMS, preferred_element_type=F32)
                st = st * dec[i * c:i * c + 1, :] + kv_t * same_head
            st_scr[...] = st
            o = jnp.concatenate(intra, axis=1) + jnp.concatenate(inter, axis=0)
            if not reverse:
                o_scr[rows, :] = o
            else:
                o = o + o_scr[rows, :]
                normed = []
                for hh in range(2):
                    oh = o[:, hh * GLA_DV:(hh + 1) * GLA_DV]
                    normed.append(oh * lax.rsqrt(jnp.mean(oh * oh, axis=-1, keepdims=True) + EPS))
                g = g_ref[0, rows, :]
                o_ref[0, rows, :] = (g * _sigmoid(g)
                                     * jnp.concatenate(normed, axis=1)).astype(o_ref.dtype)
            return carry

        lax.fori_loop(0, nblocks, recur, 0, unroll=4)

    direction(wf_ref, bf_ref, False)
    direction(wb_ref, bb_ref, True)


def _gla(proj, w_gate, b_gate, col0):
    bsz, seq_len, _ = proj.shape
    pairs = GLA_HEADS // 2
    dk2, dv2 = 2 * GLA_DK, 2 * GLA_DV
    c_q = col0
    c_k = c_q + GLA_HEADS * GLA_DK
    c_v = c_k + GLA_HEADS * GLA_DK
    c_g = c_v + GLA_HEADS * GLA_DV
    c_lr = c_g + GLA_HEADS * GLA_DV
    wg = w_gate.astype(F32).reshape(2, GLA_RANK, pairs, dk2).transpose(0, 2, 1, 3)
    wf = jnp.zeros((pairs, LANES, dk2), F32).at[:, :GLA_RANK].set(wg[0])
    wb = jnp.zeros((pairs, LANES, dk2), F32).at[:, GLA_RANK:2 * GLA_RANK].set(wg[1])

    def split3(w):
        w_hi = w.astype(BF16)
        w_mid = (w - w_hi.astype(F32)).astype(BF16)
        return jnp.concatenate([w_hi, w_hi, w_mid], axis=1)

    wf, wb = split3(wf), split3(wb)
    bg = b_gate.astype(F32).reshape(2, pairs, 1, dk2)
    tok = lambda col, width: (lambda b, p: (b, 0, col // width + p))
    mat = lambda b, p: (p, 0, 0)
    return pl.pallas_call(
        functools.partial(_gla_kernel, nblocks=seq_len // GLA_BLOCK),
        grid=(bsz, pairs),
        in_specs=[
            pl.BlockSpec((1, seq_len, dk2), tok(c_q, dk2)),
            pl.BlockSpec((1, seq_len, dk2), tok(c_k, dk2)),
            pl.BlockSpec((1, seq_len, dv2), tok(c_v, dv2)),
            pl.BlockSpec((1, seq_len, dv2), tok(c_g, dv2)),
            pl.BlockSpec((1, seq_len, LANES), lambda b, p: (b, 0, c_lr // LANES)),
            pl.BlockSpec((1, 3 * LANES, dk2), mat),
            pl.BlockSpec((1, 3 * LANES, dk2), mat),
            pl.BlockSpec((1, 1, dk2), mat),
            pl.BlockSpec((1, 1, dk2), mat),
        ],
        out_specs=pl.BlockSpec((1, seq_len, dv2), lambda b, p: (b, 0, p)),
        out_shape=jax.ShapeDtypeStruct((bsz, seq_len, GLA_HEADS * GLA_DV), BF16),
        scratch_shapes=(
            [pltpu.VMEM((seq_len, dk2), BF16)] * 4
            + [pltpu.VMEM((seq_len, dk2), F32), pltpu.VMEM((seq_len, dv2), F32),
               pltpu.VMEM((dv2, dk2), F32)]
        ),
        compiler_params=_params("parallel", "parallel"),
        name="gla",
    )(proj, proj, proj, proj, proj, wf, wb, bg[0], bg[1])


def _swap(x):
    return pltpu.roll(x, S5_STATE, 1)


def _s5_prep_kernel(lre_ref, lim_ref, ldt_ref, brr_ref, bis_ref, cr_ref, ci_ref,
                    m_ref, e_ref, ft_ref, a_ref):
    t_len = S5_CHUNK
    gsz = S5_GROUP
    width = gsz * t_len
    first = lax.broadcasted_iota(jnp.int32, (1, LANES), 1) < S5_STATE
    one_zero = jnp.where(first, 1.0, 0.0).astype(F32)
    sign = jnp.where(first, 1.0, -1.0).astype(F32)

    def packed_exp(t, are, aim):
        ph = t * aim
        return jnp.exp(t * are) * jnp.where(first, jnp.cos(ph), jnp.sin(ph))

    def dup_re(x):
        return jnp.where(first, x, _swap(x))

    def dup_im_signed(x):
        return jnp.where(first, -_swap(x), x)

    def per_channel(pw, w_a, w_b):
        return jnp.concatenate([pw * w_a[c:c + 1, :] + _swap(pw) * w_b[c:c + 1, :]
                                for c in range(gsz)], axis=0)

    trow = lax.broadcasted_iota(jnp.int32, (t_len, 1), 0).astype(F32)
    panels = []
    for d in range(2):
        lre = lre_ref[0, d:d + 1, :]
        lim = lim_ref[0, d:d + 1, :]
        dt = jnp.exp(ldt_ref[0, d:d + 1, :])
        are, aim = lre * dt, lim * dt
        num = packed_exp(jnp.ones((1, 1), F32), are, aim) - one_zero
        inv_den = 1.0 / (lre * lre + lim * lim)
        coef = num * (lre * inv_den) + _swap(num) * (lim * inv_den * sign)
        bbar = coef * brr_ref[0, d] + _swap(coef) * bis_ref[0, d]
        if d == 0:
            t_e, t_f, t_g = t_len - 1.0 - trow, trow + 1.0, trow
        else:
            t_e, t_f, t_g = trow, t_len - trow, t_len - 1.0 - trow
        e_mat = per_channel(packed_exp(t_e, are, aim), dup_re(bbar), dup_im_signed(bbar))
        e_ref[0, :, d * LANES:(d + 1) * LANES] = e_mat.astype(BF16)
        cr, ci = cr_ref[0, d], ci_ref[0, d]
        ft_ref[0, d] = per_channel(packed_exp(t_f, are, aim), cr, ci).astype(BF16)
        g_mat = per_channel(packed_exp(t_g, are, aim), cr, ci)
        panels.append(lax.dot_general(bbar, g_mat, NT_DIMS, preferred_element_type=F32,
                                      precision=HIGHEST))
        kcol = lax.broadcasted_iota(jnp.int32, (8, 1), 0)
        tk = (t_len * jnp.left_shift(1, kcol)).astype(F32)
        ak = packed_exp(tk, are, aim)
        a_ref[0, d, 0:8, :] = dup_re(ak)
        a_ref[0, d, 8:16, :] = dup_im_signed(ak)

    p_fwd, p_bwd = panels
    j = lax.broadcasted_iota(jnp.int32, (t_len, width), 0)
    pos = lax.broadcasted_iota(jnp.int32, (t_len, width), 1) % t_len
    for c in range(gsz):
        fwd = pltpu.roll(jnp.broadcast_to(p_fwd[c:c + 1, :], (t_len, width)), 0, 1,
                         stride=1, stride_axis=0)
        bwd = pltpu.roll(jnp.broadcast_to(p_bwd[c:c + 1, :], (t_len, width)), width - (t_len - 1), 1,
                         stride=1, stride_axis=0)
        blk = jnp.where(pos >= j, fwd, 0.0) + jnp.where(pos <= j, bwd, 0.0)
        m_ref[0, c * t_len:(c + 1) * t_len, :] = blk.astype(BF16)


def _s5_apply_kernel(u_ref, m_ref, e_ref, ft_ref, a_ref, d_ref, y_ref, *, blocks_per_seq, nsteps):
    t_len = S5_CHUNK
    per = LANES // t_len
    nblk = u_ref.shape[1]
    xc = [u_ref[c] for c in range(S5_GROUP)]
    ys, sums = [], []
    for q in range(per):
        u = jnp.concatenate([x[:, q * t_len:(q + 1) * t_len] for x in xc], axis=1)
        ub = u.astype(BF16)
        ys.append(jnp.dot(ub, m_ref[0], preferred_element_type=F32) + u * d_ref[0])
        sums.append(jnp.dot(ub, e_ref[0], preferred_element_type=F32))
    nidx = lax.broadcasted_iota(jnp.int32, (nblk, LANES), 0) % blocks_per_seq
    for d in range(2):

        def block_shift(val, sh):
            if d == 0:
                return jnp.where(nidx >= sh, pltpu.roll(val, sh, 0), 0.0)
            return jnp.where(nidx < blocks_per_seq - sh, pltpu.roll(val, nblk - sh, 0), 0.0)

        def chunk_shift(x, sh):
            if sh % per == 0:
                return [block_shift(v, sh // per) for v in x]
            src = [q - sh if d == 0 else q + sh for q in range(per)]
            return [x[s] if 0 <= s < per else block_shift(x[s % per], 1) for s in src]

        x = [s[:, d * LANES:(d + 1) * LANES] for s in sums]
        for k in range(nsteps):
            xs = chunk_shift(x, 1 << k)
            x = [v + a_ref[0, d, k:k + 1, :] * w + a_ref[0, d, 8 + k:9 + k, :] * _swap(w)
                 for v, w in zip(x, xs)]
        for q, v in enumerate(chunk_shift(x, 1)):
            ys[q] = ys[q] + lax.dot_general(v.astype(BF16), ft_ref[0, d], NT_DIMS,
                                            preferred_element_type=F32)
    for c in range(S5_GROUP):
        y_ref[c] = jnp.concatenate([y[:, c * t_len:(c + 1) * t_len] for y in ys], axis=1)


def _s5(su_t, seq_len, lam_re, lam_im, log_dt, b_re, b_im, c_re, c_im, d_skip):
    w, nblk, _ = su_t.shape
    g = w // S5_GROUP
    t_len = S5_CHUNK
    nchunks = seq_len // t_len
    nsteps = max(1, (nchunks - 1).bit_length())
    assert nsteps <= 8 and seq_len % LANES == 0
    width = S5_GROUP * t_len
    dup = lambda a: jnp.concatenate([a, a], axis=-1)
    gd = lambda a: jnp.moveaxis(a.astype(F32), 0, 1)
    lre = dup(gd(lam_re))
    lim = dup(gd(lam_im))
    ldt = jnp.broadcast_to(gd(log_dt)[..., None], (g, 2, LANES))
    bt_re = jnp.swapaxes(gd(b_re), -1, -2)
    bt_im = jnp.swapaxes(gd(b_im), -1, -2)
    brr = jnp.concatenate([bt_re, bt_re], axis=-1)
    bis = jnp.concatenate([-bt_im, bt_im], axis=-1)
    cre, cim = gd(c_re), gd(c_im)
    cr = jnp.concatenate([cre, -cre], axis=-1)
    ci = jnp.concatenate([-cim, -cim], axis=-1)
    vec3 = pl.BlockSpec((1, 2, LANES), lambda i: (i, 0, 0))
    mat4 = pl.BlockSpec((1, 2, S5_GROUP, LANES), lambda i: (i, 0, 0, 0))
    m_mat, e_mat, ft_mat, a_mat = pl.pallas_call(
        _s5_prep_kernel,
        grid=(g,),
        in_specs=[vec3, vec3, vec3, mat4, mat4, mat4, mat4],
        out_specs=[
            pl.BlockSpec((1, width, width), lambda i: (i, 0, 0)),
            pl.BlockSpec((1, width, 2 * LANES), lambda i: (i, 0, 0)),
            pl.BlockSpec((1, 2, width, LANES), lambda i: (i, 0, 0, 0)),
            pl.BlockSpec((1, 2, 16, LANES), lambda i: (i, 0, 0, 0)),
        ],
        out_shape=[
            jax.ShapeDtypeStruct((g, width, width), BF16),
            jax.ShapeDtypeStruct((g, width, 2 * LANES), BF16),
            jax.ShapeDtypeStruct((g, 2, width, LANES), BF16),
            jax.ShapeDtypeStruct((g, 2, 16, LANES), F32),
        ],
        compiler_params=_params("parallel"),
        name="s5_prep",
    )(lre, lim, ldt, brr, bis, cr, ci)
    d_rep = jnp.repeat(d_skip.astype(F32).reshape(g, 1, S5_GROUP), t_len, axis=-1)
    blk = pl.BlockSpec((S5_GROUP, nblk, LANES), lambda i: (i, 0, 0))
    return pl.pallas_call(
        functools.partial(_s5_apply_kernel, blocks_per_seq=seq_len // LANES, nsteps=nsteps),
        grid=(g,),
        in_specs=[
            blk,
            pl.BlockSpec((1, width, width), lambda i: (i, 0, 0)),
            pl.BlockSpec((1, width, 2 * LANES), lambda i: (i, 0, 0)),
            pl.BlockSpec((1, 2, width, LANES), lambda i: (i, 0, 0, 0)),
            pl.BlockSpec((1, 2, 16, LANES), lambda i: (i, 0, 0, 0)),
            pl.BlockSpec((1, 1, width), lambda i: (i, 0, 0)),
        ],
        out_specs=blk,
        out_shape=jax.ShapeDtypeStruct(su_t.shape, F32),
        compiler_params=_params("parallel"),
        name="s5_apply",
    )(su_t, m_mat, e_mat, ft_mat, a_mat, d_rep)


def _outproj_kernel(h_ref, gt_ref, yr_ref, ys_ref, yg_ref, wglu_ref, bglu_ref, wrg_ref, ws_ref, o_ref,
                    *, steps):
    sub = o_ref.shape[0] // LANES
    base = (pl.program_id(0) % steps) * sub
    y = jnp.concatenate([ys_ref[:, pl.ds(base + k, 1), :][:, 0, :] for k in range(sub)], axis=1)
    z = y * (0.5 * (1.0 + jnp.tanh(math.sqrt(2.0 / math.pi) * (y + 0.044715 * (y * y * y)))))
    gl = jnp.dot(wglu_ref[...], z.astype(BF16), preferred_element_type=F32) + bglu_ref[...]
    s5 = (z * _sigmoid(gl)).astype(BF16)
    acc = jnp.dot(jnp.concatenate([yr_ref[...], yg_ref[...]], axis=1), wrg_ref[...],
                  preferred_element_type=F32)
    acc = acc + lax.dot_general(s5, ws_ref[...], TN_DIMS, preferred_element_type=F32)
    o_ref[...] = h_ref[...] + gt_ref[0] * acc


def _outproj(h, seq_len, gate, y_ret, y_s5_t, y_gla, wglu_t, b_glu, w_rg, w_s, layer, *, tm=512):
    m, d = h.shape
    tm = min(tm, seq_len)
    per_batch = seq_len // tm
    steps = SUBLANES * LANES // tm
    w_s5 = y_s5_t.shape[0]
    row = lambda width: pl.BlockSpec((tm, width), lambda i: (i, 0))
    whole = lambda a: pl.BlockSpec((None,) + a.shape[1:], lambda i: (layer, 0, 0),
                                   pipeline_mode=pl.Buffered(1))
    bglu = b_glu.astype(F32).reshape(-1, w_s5, 1)
    return pl.pallas_call(
        functools.partial(_outproj_kernel, steps=steps),
        grid=(m // tm,),
        in_specs=[
            row(d),
            pl.BlockSpec((1, 1, d), lambda i: (i // per_batch, 0, 0)),
            row(y_ret.shape[1]),
            pl.BlockSpec((w_s5, SUBLANES, LANES), lambda i: (0, i // steps, 0)),
            row(y_gla.shape[1]),
            whole(wglu_t), whole(bglu), whole(w_rg), whole(w_s),
        ],
        out_specs=row(d),
        out_shape=jax.ShapeDtypeStruct((m, d), F32),
        compiler_params=_params("parallel"),
        name="outproj",
    )(h, gate, y_ret, y_s5_t, y_gla, wglu_t, bglu, w_rg, w_s)


def kernel(x, c, w_ada, b_ada, g_ffn1, ffn1_w1, ffn1_w3, ffn1_w2, g_mix, w_in, s5_lam_re, s5_lam_im, s5_log_dt, s5_b_re, s5_b_im, s5_c_re, s5_c_im, s5_d, s5_w_glu, s5_b_glu, gla_w_gate, gla_b_gate, w_out, g_ffn2, ffn2_w1, ffn2_w3, ffn2_w2, g_final):
    bsz, seq_len, d = x.shape
    depth = w_ada.shape[0]
    mod = _ada_mod(c, w_ada, b_ada).reshape(depth, bsz, N_MOD, 1, d)
    tables = _ret_tables(seq_len)
    ret_w = RET_HEADS * (2 * RET_DK + 2 * RET_DV)
    s5_w = s5_d.shape[1]
    h = x.reshape(bsz * seq_len, d)
    ffn1_w1, ffn1_w3, ffn1_w2, ffn2_w1, ffn2_w3, ffn2_w2 = (
        w.astype(BF16) for w in (ffn1_w1, ffn1_w3, ffn1_w2, ffn2_w1, ffn2_w3, ffn2_w2))
    w_in = w_in.astype(BF16)
    w_tok = jnp.concatenate([w_in[:, :, :ret_w], w_in[:, :, ret_w + s5_w:]], axis=2)
    w_tok = jnp.pad(w_tok, ((0, 0), (0, 0), (0, -w_tok.shape[2] % LANES)))
    w_s5_t = jnp.swapaxes(w_in[:, :, ret_w:ret_w + s5_w], 1, 2)
    w_out = w_out.astype(BF16)
    ret_o = RET_HEADS * RET_DV
    w_rg = jnp.concatenate([w_out[:, :ret_o], w_out[:, ret_o + s5_w:]], axis=1)
    w_s = w_out[:, ret_o:ret_o + s5_w]
    w_glu_t = jnp.swapaxes(s5_w_glu.astype(BF16), 1, 2)
    for l in range(depth):
        sh1, sc1, gt1, sh2, sc2, gt2, sh3, sc3, gt3 = (mod[l, :, i] for i in range(N_MOD))
        h = _ffn(h, seq_len, g_ffn1[l], sh1, sc1, gt1, ffn1_w1, ffn1_w3, ffn1_w2, l)
        proj, su_t = _inproj(h, seq_len, g_mix[l], sh2, sc2, w_tok, w_s5_t, l)
        proj = proj.reshape(bsz, seq_len, -1)
        y_ret = _retention(proj, tables)
        y_s5_t = _s5(su_t, seq_len, s5_lam_re[l], s5_lam_im[l], s5_log_dt[l],
                     s5_b_re[l], s5_b_im[l], s5_c_re[l], s5_c_im[l], s5_d[l])
        y_gla = _gla(proj, gla_w_gate[l], gla_b_gate[l], ret_w)
        h = _outproj(h, seq_len, gt2, y_ret.reshape(bsz * seq_len, -1), y_s5_t,
                     y_gla.reshape(bsz * seq_len, -1), w_glu_t, s5_b_glu, w_rg, w_s, l)
        h = _ffn(h, seq_len, g_ffn2[l], sh3, sc3, gt3, ffn2_w1, ffn2_w3, ffn2_w2, l,
                 g_final if l == depth - 1 else None)
    return h.reshape(bsz, seq_len, d)
```

```python
import functools
import math

import jax
import jax.numpy as jnp
from jax import lax
from jax.experimental import pallas as pl
from jax.experimental.pallas import tpu as pltpu

F32 = jnp.float32
BF16 = jnp.bfloat16
HIGHEST = lax.Precision.HIGHEST

RET_HEADS = 4
RET_DK = 128
RET_DV = 256
RET_CHUNK = 256
ROPE_BASE = 10000.0
S5_GROUP = 16
S5_STATE = 64
S5_CHUNK = 32
GLA_HEADS = 4
GLA_DK = 64
GLA_DV = 128
GLA_RANK = 16
GLA_TAU = 16.0
GLA_CHUNK = 64
GLA_BLOCK = 256
FFN_RES = 0.5
FFN_ROWS = 512
N_MOD = 9
EPS = 1e-6

LANES = 128
SUBLANES = 8
MXU_N = 256
VMEM_LIMIT = 60 << 20

NT_DIMS = (((1,), (1,)), ((), ()))
TN_DIMS = (((0,), (0,)), ((), ()))


def _params(*semantics):
    return pltpu.CompilerParams(dimension_semantics=semantics, vmem_limit_bytes=VMEM_LIMIT)


def _sigmoid(x):
    return 1.0 / (1.0 + jnp.exp(-x))


def _norm_mod(x, g, shift, scale):
    ms = jnp.mean(x * x, axis=-1, keepdims=True)
    return x * lax.rsqrt(ms + EPS) * (g * (1.0 + scale)) + shift


def _ada_kernel(c_ref, w_ref, b_ref, o_ref):
    c = c_ref[...]
    cond = c * _sigmoid(c)
    rows = cond.shape[0]
    c_hi = cond.astype(BF16).astype(F32)
    lhs = jnp.concatenate([c_hi, cond - c_hi], axis=0).astype(BF16)
    w = w_ref[0]
    w_hi = w.astype(BF16)
    w_mid = (w - w_hi.astype(F32)).astype(BF16)
    s = (jnp.dot(lhs, w_hi, preferred_element_type=F32)
         + jnp.dot(lhs, w_mid, preferred_element_type=F32))
    o_ref[0] = s[:rows] + s[rows:] + b_ref[0]


def _ada_mod(c, w_ada, b_ada):
    depth, d, n = w_ada.shape
    bsz = c.shape[0]
    rows = -(-bsz // 8) * 8
    c_pad = jnp.zeros((rows, d), F32).at[:bsz].set(c)
    tn = n // 16
    out = pl.pallas_call(
        _ada_kernel,
        grid=(depth, n // tn),
        in_specs=[
            pl.BlockSpec((rows, d), lambda l, j: (0, 0)),
            pl.BlockSpec((1, d, tn), lambda l, j: (l, 0, j)),
            pl.BlockSpec((1, 1, tn), lambda l, j: (l, 0, j)),
        ],
        out_specs=pl.BlockSpec((1, rows, tn), lambda l, j: (l, 0, j)),
        out_shape=jax.ShapeDtypeStruct((depth, rows, n), F32),
        compiler_params=_params("parallel", "parallel"),
        name="ada_mod",
    )(c_pad, w_ada, b_ada.reshape(depth, 1, n))
    return out[:, :bsz]


def _ffn_kernel(h_ref, g_ref, sh_ref, sc_ref, gt_ref, w1_ref, w3_ref, w2_ref, *rest, tf, nj, final):
    if final:
        gf_ref, o_ref, u_scr = rest
    else:
        o_ref, u_scr = rest
    j = pl.program_id(1)

    row_blocks = [slice(r0, r0 + FFN_ROWS) for r0 in range(0, o_ref.shape[0], FFN_ROWS)]

    def normalize(rows):
        u_scr[rows, :] = _norm_mod(h_ref[rows, :], g_ref[...], sh_ref[0], sc_ref[0]).astype(BF16)

    def matmuls(rows, first):
        u = u_scr[rows, :]
        acts = []
        for c0 in range(0, tf, MXU_N):
            a = jnp.dot(u, w1_ref[:, c0:c0 + MXU_N], preferred_element_type=F32)
            b = jnp.dot(u, w3_ref[:, c0:c0 + MXU_N], preferred_element_type=F32)
            acts.append((a * _sigmoid(a) * b).astype(BF16))
        down = jnp.dot(jnp.concatenate(acts, axis=1), w2_ref[...], preferred_element_type=F32)
        o_ref[rows, :] = down if first else o_ref[rows, :] + down

    def finish(rows):
        out = h_ref[rows, :] + (FFN_RES * gt_ref[0]) * o_ref[rows, :]
        if final:
            ms = jnp.mean(out * out, axis=-1, keepdims=True)
            out = out * lax.rsqrt(ms + EPS) * gf_ref[...]
        o_ref[rows, :] = out

    @pl.when(j == 0)
    def _():
        for rows in row_blocks:
            normalize(rows)
            matmuls(rows, True)

    @pl.when(jnp.logical_and(j > 0, j < nj - 1))
    def _():
        for rows in row_blocks:
            matmuls(rows, False)

    @pl.when(j == nj - 1)
    def _():
        for rows in row_blocks:
            matmuls(rows, False)
            finish(rows)


def _ffn(h, seq_len, g, shift, scale, gate, w1, w3, w2, layer, g_final=None, *, tm=1024, tf=512):
    m, d = h.shape
    dff = w1.shape[2]
    tm = min(tm, seq_len)
    nj = dff // tf
    assert nj >= 2 and tm % FFN_ROWS == 0
    per_batch = seq_len // tm
    final = g_final is not None
    vec = lambda i, j: (i // per_batch, 0, 0)
    in_specs = [
        pl.BlockSpec((tm, d), lambda i, j: (i, 0)),
        pl.BlockSpec((1, d), lambda i, j: (0, 0)),
        pl.BlockSpec((1, 1, d), vec),
        pl.BlockSpec((1, 1, d), vec),
        pl.BlockSpec((1, 1, d), vec),
        pl.BlockSpec((None, d, tf), lambda i, j: (layer, 0, j)),
        pl.BlockSpec((None, d, tf), lambda i, j: (layer, 0, j)),
        pl.BlockSpec((None, tf, d), lambda i, j: (layer, j, 0)),
    ]
    args = [h, g.reshape(1, d), shift, scale, gate, w1, w3, w2]
    if final:
        in_specs.append(pl.BlockSpec((1, d), lambda i, j: (0, 0)))
        args.append(g_final.reshape(1, d))
    return pl.pallas_call(
        functools.partial(_ffn_kernel, tf=tf, nj=nj, final=final),
        grid=(m // tm, nj),
        in_specs=in_specs,
        out_specs=pl.BlockSpec((tm, d), lambda i, j: (i, 0)),
        out_shape=jax.ShapeDtypeStruct((m, d), F32),
        scratch_shapes=[pltpu.VMEM((tm, d), BF16)],
        compiler_params=_params("parallel", "arbitrary"),
        name="ffn",
    )(*args)


def _inproj_kernel(h_ref, g_ref, sh_ref, sc_ref, w_ref, wst_ref, o_ref, ost_ref, *, steps):
    u = _norm_mod(h_ref[...], g_ref[...], sh_ref[0], sc_ref[0]).astype(BF16)
    o_ref[...] = jnp.dot(u, w_ref[...], preferred_element_type=F32)
    su_t = lax.dot_general(wst_ref[...], u, NT_DIMS, preferred_element_type=F32)
    sub = su_t.shape[1] // LANES
    base = (pl.program_id(0) % steps) * sub
    for k in range(sub):
        ost_ref[:, pl.ds(base + k, 1), :] = su_t[:, k * LANES:(k + 1) * LANES][:, None, :]


def _inproj(h, seq_len, g, shift, scale, w, w_s5_t, layer, *, tm=256):
    m, d = h.shape
    d_pad = w.shape[2]
    w_s5 = w_s5_t.shape[1]
    tm = min(tm, seq_len)
    per_batch = seq_len // tm
    steps = SUBLANES * LANES // tm
    vec = lambda i: (i // per_batch, 0, 0)
    return pl.pallas_call(
        functools.partial(_inproj_kernel, steps=steps),
        grid=(m // tm,),
        in_specs=[
            pl.BlockSpec((tm, d), lambda i: (i, 0)),
            pl.BlockSpec((1, d), lambda i: (0, 0)),
            pl.BlockSpec((1, 1, d), vec),
            pl.BlockSpec((1, 1, d), vec),
            pl.BlockSpec((None, d, d_pad), lambda i: (layer, 0, 0), pipeline_mode=pl.Buffered(1)),
            pl.BlockSpec((None, w_s5, d), lambda i: (layer, 0, 0), pipeline_mode=pl.Buffered(1)),
        ],
        out_specs=[pl.BlockSpec((tm, d_pad), lambda i: (i, 0)),
                   pl.BlockSpec((w_s5, SUBLANES, LANES), lambda i: (0, i // steps, 0))],
        out_shape=[jax.ShapeDtypeStruct((m, d_pad), F32),
                   jax.ShapeDtypeStruct((w_s5, m // LANES, LANES), F32)],
        compiler_params=_params("arbitrary"),
        name="inproj",
    )(h, g.reshape(1, d), shift, scale, w, w_s5_t)


def _ret_tables(seq_len):
    c = RET_CHUNK
    pos = jnp.arange(seq_len, dtype=F32)
    inv_freq = ROPE_BASE ** (-jnp.arange(0, RET_DK, 2, dtype=F32) / RET_DK)
    ang = pos[:, None] * inv_freq[None, :]
    cos, sin = jnp.cos(ang), jnp.sin(ang)
    cs = jnp.concatenate([cos, cos], axis=-1)
    sn = jnp.concatenate([-sin, sin], axis=-1)
    lg = jnp.log1p(-jnp.exp2(-5.0 - jnp.arange(RET_HEADS, dtype=F32)))
    idx = jnp.arange(c, dtype=F32)
    dist = jnp.abs(idx[:, None] - idx[None, :])
    dmat = jnp.exp(dist[None] * lg[:, None, None])
    rows = jnp.stack([c - 1.0 - idx, idx, idx + 1.0, c - idx])
    dvec = jnp.exp(rows[None] * lg[:, None, None])
    dvec = jnp.broadcast_to(dvec[..., None], dvec.shape + (RET_DK,))
    gc = jnp.broadcast_to(jnp.exp(c * lg)[:, None, None], (RET_HEADS, 1, RET_DV))
    return cs, sn, dmat, dvec, gc


def _ret_kernel(q_ref, k_ref, v_ref, g_ref, cs_ref, sn_ref, dmat_ref, dvec_ref, gc_ref, o_ref,
                q_scr, k_scr, qf_scr, qb_scr, kf_scr, kb_scr, of_scr, ob_scr, sf_scr, sb_scr,
                *, nchunks):
    c = RET_CHUNK
    scale = RET_DK ** -0.5
    chunk_rows = lambda n: pl.ds(pl.multiple_of(n * c, c), c)

    def prepare(n, carry):
        rows = chunk_rows(n)
        cs = cs_ref[rows, :]
        sn = sn_ref[rows, :]
        q = q_ref[0, rows, :]
        k = k_ref[0, rows, :]
        q = (q * cs + pltpu.roll(q, RET_DK // 2, 1) * sn) * scale
        k = k * cs + pltpu.roll(k, RET_DK // 2, 1) * sn
        q_scr[rows, :] = q.astype(BF16)
        k_scr[rows, :] = k.astype(BF16)
        kf_scr[rows, :] = (k * dvec_ref[0, 0]).astype(BF16)
        kb_scr[rows, :] = (k * dvec_ref[0, 1]).astype(BF16)
        qf_scr[rows, :] = (q * dvec_ref[0, 2]).astype(BF16)
        qb_scr[rows, :] = (q * dvec_ref[0, 3]).astype(BF16)
        return carry

    lax.fori_loop(0, nchunks, prepare, 0, unroll=4)
    sf_scr[...] = jnp.zeros_like(sf_scr)
    sb_scr[...] = jnp.zeros_like(sb_scr)
    gc = gc_ref[0]

    def recur(n, carry):
        rf = chunk_rows(n)
        vf = v_ref[0, rf, :].astype(BF16)
        scores = lax.dot_general(q_scr[rf, :], k_scr[rf, :], NT_DIMS,
                                 preferred_element_type=F32) * dmat_ref[0]
        s = sf_scr[...]
        of_scr[rf, :] = (jnp.dot(scores.astype(BF16), vf, preferred_element_type=F32)
                         + jnp.dot(qf_scr[rf, :], s.astype(BF16), preferred_element_type=F32))
        sf_scr[...] = gc * s + lax.dot_general(kf_scr[rf, :], vf, TN_DIMS,
                                               preferred_element_type=F32)
        rb = chunk_rows(nchunks - 1 - n)
        vb = v_ref[0, rb, :].astype(BF16)
        s = sb_scr[...]
        ob_scr[rb, :] = jnp.dot(qb_scr[rb, :], s.astype(BF16), preferred_element_type=F32)
        sb_scr[...] = gc * s + lax.dot_general(kb_scr[rb, :], vb, TN_DIMS,
                                               preferred_element_type=F32)
        return carry

    lax.fori_loop(0, nchunks, recur, 0, unroll=4)

    def finish(n, carry):
        rows = chunk_rows(n)
        o = of_scr[rows, :] + ob_scr[rows, :]
        oc = o - jnp.mean(o, axis=-1, keepdims=True)
        ln = oc * lax.rsqrt(jnp.mean(oc * oc, axis=-1, keepdims=True) + EPS)
        g = g_ref[0, rows, :]
        o_ref[0, rows, :] = (g * _sigmoid(g) * ln).astype(o_ref.dtype)
        return carry

    lax.fori_loop(0, nchunks, finish, 0, unroll=4)


def _retention(proj, tables):
    bsz, seq_len, _ = proj.shape
    cs, sn, dmat, dvec, gc = tables
    h = RET_HEADS
    kq, kk = 0, h
    kv, kg = (2 * h * RET_DK) // RET_DV, (2 * h * RET_DK + h * RET_DV) // RET_DV
    tok = lambda off: (lambda b, i: (b, 0, off + i))
    return pl.pallas_call(
        functools.partial(_ret_kernel, nchunks=seq_len // RET_CHUNK),
        grid=(bsz, h),
        in_specs=[
            pl.BlockSpec((1, seq_len, RET_DK), tok(kq)),
            pl.BlockSpec((1, seq_len, RET_DK), tok(kk)),
            pl.BlockSpec((1, seq_len, RET_DV), tok(kv)),
            pl.BlockSpec((1, seq_len, RET_DV), tok(kg)),
            pl.BlockSpec((seq_len, RET_DK), lambda b, i: (0, 0), pipeline_mode=pl.Buffered(1)),
            pl.BlockSpec((seq_len, RET_DK), lambda b, i: (0, 0), pipeline_mode=pl.Buffered(1)),
            pl.BlockSpec((1, RET_CHUNK, RET_CHUNK), lambda b, i: (i, 0, 0)),
            pl.BlockSpec((1, 4, RET_CHUNK, RET_DK), lambda b, i: (i, 0, 0, 0)),
            pl.BlockSpec((1, 1, RET_DV), lambda b, i: (i, 0, 0)),
        ],
        out_specs=pl.BlockSpec((1, seq_len, RET_DV), lambda b, i: (b, 0, i)),
        out_shape=jax.ShapeDtypeStruct((bsz, seq_len, h * RET_DV), BF16),
        scratch_shapes=(
            [pltpu.VMEM((seq_len, RET_DK), BF16)] * 6
            + [pltpu.VMEM((seq_len, RET_DV), F32)] * 2
            + [pltpu.VMEM((RET_DK, RET_DV), F32)] * 2
        ),
        compiler_params=_params("parallel", "parallel"),
        name="retention",
    )(proj, proj, proj, proj, cs, sn, dmat, dvec, gc)


def _gla_kernel(q_ref, k_ref, v_ref, g_ref, lr_ref, wf_ref, wb_ref, bf_ref, bb_ref, o_ref,
                q0_scr, q1_scr, kin_scr, kst_scr, dec_scr, o_scr, st_scr, *, nblocks):
    c, r = GLA_CHUNK, GLA_BLOCK
    per = r // c
    dk2, dv2 = 2 * GLA_DK, 2 * GLA_DV
    scale = GLA_DK ** -0.5
    block_rows = lambda m: pl.ds(pl.multiple_of(m * r, r), r)

    ri = lax.broadcasted_iota(jnp.int32, (r, r), 0)
    ci = lax.broadcasted_iota(jnp.int32, (r, r), 1)
    same_chunk = (ri // c) == (ci // c)
    lane = lax.broadcasted_iota(jnp.int32, (r, dk2), 1)
    v_lane = lax.broadcasted_iota(jnp.int32, (r, dv2), 1)
    se = lax.broadcasted_iota(jnp.int32, (dv2, dk2), 0)
    sd = lax.broadcasted_iota(jnp.int32, (dv2, dk2), 1)
    same_head = jnp.where(se < GLA_DV, jnp.where(sd < GLA_DK, 1.0, 0.0),
                          jnp.where(sd >= GLA_DK, 1.0, 0.0)).astype(F32)

    def direction(w_ref, b_ref, reverse):
        if reverse:
            cum = same_chunk & (ci >= ri)
            keep = same_chunk & (ci > ri)
        else:
            cum = same_chunk & (ci <= ri)
            keep = same_chunk & (ci <= ri)
        sum_mat = jnp.concatenate([jnp.where(cum, 1.0, 0.0), jnp.where(same_chunk, 1.0, 0.0)],
                                  axis=0).astype(BF16)

        def prepare(m, carry):
            rows = block_rows(m)
            lr = lr_ref[0, rows, :]
            lr_hi = lr.astype(BF16)
            lr_mid = (lr - lr_hi.astype(F32)).astype(BF16)
            x = jnp.dot(jnp.concatenate([lr_hi, lr_mid, lr_hi], axis=1), w_ref[0],
                        preferred_element_type=F32) + b_ref[0]
            la = (jnp.minimum(x, 0.0) - jnp.log(1.0 + jnp.exp(-jnp.abs(x)))) * (1.0 / GLA_TAU)
            hi = la.astype(BF16)
            rest = la - hi.astype(F32)
            mid = rest.astype(BF16)
            lo = (rest - mid.astype(F32)).astype(BF16)
            sums = jnp.dot(sum_mat, jnp.concatenate([hi, mid, lo], axis=1),
                           preferred_element_type=F32)
            sums = sums[:, :dk2] + sums[:, dk2:2 * dk2] + sums[:, 2 * dk2:]
            b, total = sums[:r], sums[r:]
            q_in = q_ref[0, rows, :] * scale * jnp.exp(b)
            k = k_ref[0, rows, :]
            q0_scr[rows, :] = jnp.where(lane < GLA_DK, q_in, 0.0).astype(BF16)
            q1_scr[rows, :] = jnp.where(lane >= GLA_DK, q_in, 0.0).astype(BF16)
            kin_scr[rows, :] = (k * jnp.exp(-b)).astype(BF16)
            kst_scr[rows, :] = (k * jnp.exp(total - b)).astype(BF16)
            dec_scr[rows, :] = jnp.exp(total)
            return carry

        lax.fori_loop(0, nblocks, prepare, 0, unroll=4)
        st_scr[...] = jnp.zeros_like(st_scr)

        def recur(t, carry):
            rows = block_rows(nblocks - 1 - t if reverse else t)
            q0 = q0_scr[rows, :]
            q1 = q1_scr[rows, :]
            kin = kin_scr[rows, :]
            kst = kst_scr[rows, :]
            dec = dec_scr[rows, :]
            vb = v_ref[0, rows, :].astype(BF16)
            scores = [jnp.where(keep, lax.dot_general(qh, kin, NT_DIMS, preferred_element_type=F32),
                                0.0).astype(BF16) for qh in (q0, q1)]
            v_heads = jnp.concatenate([jnp.where(v_lane < GLA_DV, vb, jnp.zeros_like(vb)),
                                       jnp.where(v_lane >= GLA_DV, vb, jnp.zeros_like(vb))], axis=0)
            intra = jnp.dot(jnp.concatenate(scores, axis=1), v_heads, preferred_element_type=F32)
            qin = q0 + q1
            st = st_scr[...]
            inter = [None] * per
            for i in (range(per - 1, -1, -1) if reverse else range(per)):
                sl = slice(i * c, (i + 1) * c)
                inter[i] = lax.dot_general(qin[sl], st.astype(BF16), NT_DIMS,
                                           preferred_element_type=F32)
                kv_t = lax.dot_general(vb[sl], kst[sl], TN_DIMS, preferred_element_type=F32)
                st = st * dec[i * c:i * c + 1, :] + kv_t * same_head
            st_scr[...] = st
            o = intra + jnp.concatenate(inter, axis=0)
            if not reverse:
                o_scr[rows, :] = o
            else:
                o = o + o_scr[rows, :]
                normed = []
                for hh in range(2):
                    oh = o[:, hh * GLA_DV:(hh + 1) * GLA_DV]
                    normed.append(oh * lax.rsqrt(jnp.mean(oh * oh, axis=-1, keepdims=True) + EPS))
                g = g_ref[0, rows, :]
                o_ref[0, rows, :] = (g * _sigmoid(g)
                                     * jnp.concatenate(normed, axis=1)).astype(o_ref.dtype)
            return carry

        lax.fori_loop(0, nblocks, recur, 0, unroll=4)

    direction(wf_ref, bf_ref, False)
    direction(wb_ref, bb_ref, True)


def _gla(proj, w_gate, b_gate, col0):
    bsz, seq_len, _ = proj.shape
    pairs = GLA_HEADS // 2
    dk2, dv2 = 2 * GLA_DK, 2 * GLA_DV
    c_q = col0
    c_k = c_q + GLA_HEADS * GLA_DK
    c_v = c_k + GLA_HEADS * GLA_DK
    c_g = c_v + GLA_HEADS * GLA_DV
    c_lr = c_g + GLA_HEADS * GLA_DV
    wg = w_gate.astype(F32).reshape(2, GLA_RANK, pairs, dk2).transpose(0, 2, 1, 3)
    wf = jnp.zeros((pairs, LANES, dk2), F32).at[:, :GLA_RANK].set(wg[0])
    wb = jnp.zeros((pairs, LANES, dk2), F32).at[:, GLA_RANK:2 * GLA_RANK].set(wg[1])

    def split3(w):
        w_hi = w.astype(BF16)
        w_mid = (w - w_hi.astype(F32)).astype(BF16)
        return jnp.concatenate([w_hi, w_hi, w_mid], axis=1)

    wf, wb = split3(wf), split3(wb)
    bg = b_gate.astype(F32).reshape(2, pairs, 1, dk2)
    tok = lambda col, width: (lambda b, p: (b, 0, col // width + p))
    mat = lambda b, p: (p, 0, 0)
    return pl.pallas_call(
        functools.partial(_gla_kernel, nblocks=seq_len // GLA_BLOCK),
        grid=(bsz, pairs),
        in_specs=[
            pl.BlockSpec((1, seq_len, dk2), tok(c_q, dk2)),
            pl.BlockSpec((1, seq_len, dk2), tok(c_k, dk2)),
            pl.BlockSpec((1, seq_len, dv2), tok(c_v, dv2)),
            pl.BlockSpec((1, seq_len, dv2), tok(c_g, dv2)),
            pl.BlockSpec((1, seq_len, LANES), lambda b, p: (b, 0, c_lr // LANES)),
            pl.BlockSpec((1, 3 * LANES, dk2), mat),
            pl.BlockSpec((1, 3 * LANES, dk2), mat),
            pl.BlockSpec((1, 1, dk2), mat),
            pl.BlockSpec((1, 1, dk2), mat),
        ],
        out_specs=pl.BlockSpec((1, seq_len, dv2), lambda b, p: (b, 0, p)),
        out_shape=jax.ShapeDtypeStruct((bsz, seq_len, GLA_HEADS * GLA_DV), BF16),
        scratch_shapes=(
            [pltpu.VMEM((seq_len, dk2), BF16)] * 4
            + [pltpu.VMEM((seq_len, dk2), F32), pltpu.VMEM((seq_len, dv2), F32),
               pltpu.VMEM((dv2, dk2), F32)]
        ),
        compiler_params=_params("parallel", "parallel"),
        name="gla",
    )(proj, proj, proj, proj, proj, wf, wb, bg[0], bg[1])


def _swap(x):
    return pltpu.roll(x, S5_STATE, 1)


def _s5_prep_kernel(lre_ref, lim_ref, ldt_ref, brr_ref, bis_ref, cr_ref, ci_ref,
                    m_ref, e_ref, ft_ref, a_ref):
    t_len = S5_CHUNK
    gsz = S5_GROUP
    width = gsz * t_len
    first = lax.broadcasted_iota(jnp.int32, (1, LANES), 1) < S5_STATE
    one_zero = jnp.where(first, 1.0, 0.0).astype(F32)
    sign = jnp.where(first, 1.0, -1.0).astype(F32)

    def packed_exp(t, are, aim):
        ph = t * aim
        return jnp.exp(t * are) * jnp.where(first, jnp.cos(ph), jnp.sin(ph))

    def dup_re(x):
        return jnp.where(first, x, _swap(x))

    def dup_im_signed(x):
        return jnp.where(first, -_swap(x), x)

    def per_channel(pw, w_a, w_b):
        return jnp.concatenate([pw * w_a[c:c + 1, :] + _swap(pw) * w_b[c:c + 1, :]
                                for c in range(gsz)], axis=0)

    trow = lax.broadcasted_iota(jnp.int32, (t_len, 1), 0).astype(F32)
    panels = []
    for d in range(2):
        lre = lre_ref[0, d:d + 1, :]
        lim = lim_ref[0, d:d + 1, :]
        dt = jnp.exp(ldt_ref[0, d:d + 1, :])
        are, aim = lre * dt, lim * dt
        num = packed_exp(jnp.ones((1, 1), F32), are, aim) - one_zero
        inv_den = 1.0 / (lre * lre + lim * lim)
        coef = num * (lre * inv_den) + _swap(num) * (lim * inv_den * sign)
        bbar = coef * brr_ref[0, d] + _swap(coef) * bis_ref[0, d]
        if d == 0:
            t_e, t_f, t_g = t_len - 1.0 - trow, trow + 1.0, trow
        else:
            t_e, t_f, t_g = trow, t_len - trow, t_len - 1.0 - trow
        e_mat = per_channel(packed_exp(t_e, are, aim), dup_re(bbar), dup_im_signed(bbar))
        e_ref[0, :, d * LANES:(d + 1) * LANES] = e_mat.astype(BF16)
        cr, ci = cr_ref[0, d], ci_ref[0, d]
        ft_ref[0, d] = per_channel(packed_exp(t_f, are, aim), cr, ci).astype(BF16)
        g_mat = per_channel(packed_exp(t_g, are, aim), cr, ci)
        panels.append(lax.dot_general(bbar, g_mat, NT_DIMS, preferred_element_type=F32,
                                      precision=HIGHEST))
        kcol = lax.broadcasted_iota(jnp.int32, (8, 1), 0)
        tk = (t_len * jnp.left_shift(1, kcol)).astype(F32)
        ak = packed_exp(tk, are, aim)
        a_ref[0, d, 0:8, :] = dup_re(ak)
        a_ref[0, d, 8:16, :] = dup_im_signed(ak)

    p_fwd, p_bwd = panels
    j = lax.broadcasted_iota(jnp.int32, (t_len, width), 0)
    pos = lax.broadcasted_iota(jnp.int32, (t_len, width), 1) % t_len
    for c in range(gsz):
        fwd = pltpu.roll(jnp.broadcast_to(p_fwd[c:c + 1, :], (t_len, width)), 0, 1,
                         stride=1, stride_axis=0)
        bwd = pltpu.roll(jnp.broadcast_to(p_bwd[c:c + 1, :], (t_len, width)), width - (t_len - 1), 1,
                         stride=1, stride_axis=0)
        blk = jnp.where(pos >= j, fwd, 0.0) + jnp.where(pos <= j, bwd, 0.0)
        m_ref[0, c * t_len:(c + 1) * t_len, :] = blk.astype(BF16)


def _s5_apply_kernel(u_ref, m_ref, e_ref, ft_ref, a_ref, d_ref, y_ref, *, blocks_per_seq, nsteps):
    t_len = S5_CHUNK
    per = LANES // t_len
    nblk = u_ref.shape[1]
    xc = [u_ref[c] for c in range(S5_GROUP)]
    ys, sums = [], []
    for q in range(per):
        u = jnp.concatenate([x[:, q * t_len:(q + 1) * t_len] for x in xc], axis=1)
        ub = u.astype(BF16)
        ys.append(jnp.dot(ub, m_ref[0], preferred_element_type=F32) + u * d_ref[0])
        sums.append(jnp.dot(ub, e_ref[0], preferred_element_type=F32))
    nidx = lax.broadcasted_iota(jnp.int32, (nblk, LANES), 0) % blocks_per_seq
    for d in range(2):

        def block_shift(val, sh):
            if d == 0:
                return jnp.where(nidx >= sh, pltpu.roll(val, sh, 0), 0.0)
            return jnp.where(nidx < blocks_per_seq - sh, pltpu.roll(val, nblk - sh, 0), 0.0)

        def chunk_shift(x, sh):
            if sh % per == 0:
                return [block_shift(v, sh // per) for v in x]
            src = [q - sh if d == 0 else q + sh for q in range(per)]
            return [x[s] if 0 <= s < per else block_shift(x[s % per], 1) for s in src]

        x = [s[:, d * LANES:(d + 1) * LANES] for s in sums]
        for k in range(nsteps):
            xs = chunk_shift(x, 1 << k)
            x = [v + a_ref[0, d, k:k + 1, :] * w + a_ref[0, d, 8 + k:9 + k, :] * _swap(w)
                 for v, w in zip(x, xs)]
        for q, v in enumerate(chunk_shift(x, 1)):
            ys[q] = ys[q] + lax.dot_general(v.astype(BF16), ft_ref[0, d], NT_DIMS,
                                            preferred_element_type=F32)
    for c in range(S5_GROUP):
        y_ref[c] = jnp.concatenate([y[:, c * t_len:(c + 1) * t_len] for y in ys], axis=1)


def _s5(su_t, seq_len, lam_re, lam_im, log_dt, b_re, b_im, c_re, c_im, d_skip):
    w, nblk, _ = su_t.shape
    g = w // S5_GROUP
    t_len = S5_CHUNK
    nchunks = seq_len // t_len
    nsteps = max(1, (nchunks - 1).bit_length())
    assert nsteps <= 8 and seq_len % LANES == 0
    width = S5_GROUP * t_len
    dup = lambda a: jnp.concatenate([a, a], axis=-1)
    gd = lambda a: jnp.moveaxis(a.astype(F32), 0, 1)
    lre = dup(gd(lam_re))
    lim = dup(gd(lam_im))
    ldt = jnp.broadcast_to(gd(log_dt)[..., None], (g, 2, LANES))
    bt_re = jnp.swapaxes(gd(b_re), -1, -2)
    bt_im = jnp.swapaxes(gd(b_im), -1, -2)
    brr = jnp.concatenate([bt_re, bt_re], axis=-1)
    bis = jnp.concatenate([-bt_im, bt_im], axis=-1)
    cre, cim = gd(c_re), gd(c_im)
    cr = jnp.concatenate([cre, -cre], axis=-1)
    ci = jnp.concatenate([-cim, -cim], axis=-1)
    vec3 = pl.BlockSpec((1, 2, LANES), lambda i: (i, 0, 0))
    mat4 = pl.BlockSpec((1, 2, S5_GROUP, LANES), lambda i: (i, 0, 0, 0))
    m_mat, e_mat, ft_mat, a_mat = pl.pallas_call(
        _s5_prep_kernel,
        grid=(g,),
        in_specs=[vec3, vec3, vec3, mat4, mat4, mat4, mat4],
        out_specs=[
            pl.BlockSpec((1, width, width), lambda i: (i, 0, 0)),
            pl.BlockSpec((1, width, 2 * LANES), lambda i: (i, 0, 0)),
            pl.BlockSpec((1, 2, width, LANES), lambda i: (i, 0, 0, 0)),
            pl.BlockSpec((1, 2, 16, LANES), lambda i: (i, 0, 0, 0)),
        ],
        out_shape=[
            jax.ShapeDtypeStruct((g, width, width), BF16),
            jax.ShapeDtypeStruct((g, width, 2 * LANES), BF16),
            jax.ShapeDtypeStruct((g, 2, width, LANES), BF16),
            jax.ShapeDtypeStruct((g, 2, 16, LANES), F32),
        ],
        compiler_params=_params("parallel"),
        name="s5_prep",
    )(lre, lim, ldt, brr, bis, cr, ci)
    d_rep = jnp.repeat(d_skip.astype(F32).reshape(g, 1, S5_GROUP), t_len, axis=-1)
    blk = pl.BlockSpec((S5_GROUP, nblk, LANES), lambda i: (i, 0, 0))
    return pl.pallas_call(
        functools.partial(_s5_apply_kernel, blocks_per_seq=seq_len // LANES, nsteps=nsteps),
        grid=(g,),
        in_specs=[
            blk,
            pl.BlockSpec((1, width, width), lambda i: (i, 0, 0)),
            pl.BlockSpec((1, width, 2 * LANES), lambda i: (i, 0, 0)),
            pl.BlockSpec((1, 2, width, LANES), lambda i: (i, 0, 0, 0)),
            pl.BlockSpec((1, 2, 16, LANES), lambda i: (i, 0, 0, 0)),
            pl.BlockSpec((1, 1, width), lambda i: (i, 0, 0)),
        ],
        out_specs=blk,
        out_shape=jax.ShapeDtypeStruct(su_t.shape, F32),
        compiler_params=_params("parallel"),
        name="s5_apply",
    )(su_t, m_mat, e_mat, ft_mat, a_mat, d_rep)


def _outproj_kernel(h_ref, gt_ref, yr_ref, ys_ref, yg_ref, wglu_ref, bglu_ref, wrg_ref, ws_ref, o_ref,
                    *, steps):
    sub = o_ref.shape[0] // LANES
    base = (pl.program_id(0) % steps) * sub
    y = jnp.concatenate([ys_ref[:, pl.ds(base + k, 1), :][:, 0, :] for k in range(sub)], axis=1)
    z = y * (0.5 * (1.0 + jnp.tanh(math.sqrt(2.0 / math.pi) * (y + 0.044715 * (y * y * y)))))
    gl = jnp.dot(wglu_ref[...], z.astype(BF16), preferred_element_type=F32) + bglu_ref[...]
    s5 = (z * _sigmoid(gl)).astype(BF16)
    acc = jnp.dot(jnp.concatenate([yr_ref[...], yg_ref[...]], axis=1), wrg_ref[...],
                  preferred_element_type=F32)
    acc = acc + lax.dot_general(s5, ws_ref[...], TN_DIMS, preferred_element_type=F32)
    o_ref[...] = h_ref[...] + gt_ref[0] * acc


def _outproj(h, seq_len, gate, y_ret, y_s5_t, y_gla, wglu_t, b_glu, w_rg, w_s, layer, *, tm=512):
    m, d = h.shape
    tm = min(tm, seq_len)
    per_batch = seq_len // tm
    steps = SUBLANES * LANES // tm
    w_s5 = y_s5_t.shape[0]
    row = lambda width: pl.BlockSpec((tm, width), lambda i: (i, 0))
    whole = lambda a: pl.BlockSpec((None,) + a.shape[1:], lambda i: (layer, 0, 0),
                                   pipeline_mode=pl.Buffered(1))
    bglu = b_glu.astype(F32).reshape(-1, w_s5, 1)
    return pl.pallas_call(
        functools.partial(_outproj_kernel, steps=steps),
        grid=(m // tm,),
        in_specs=[
            row(d),
            pl.BlockSpec((1, 1, d), lambda i: (i // per_batch, 0, 0)),
            row(y_ret.shape[1]),
            pl.BlockSpec((w_s5, SUBLANES, LANES), lambda i: (0, i // steps, 0)),
            row(y_gla.shape[1]),
            whole(wglu_t), whole(bglu), whole(w_rg), whole(w_s),
        ],
        out_specs=row(d),
        out_shape=jax.ShapeDtypeStruct((m, d), F32),
        compiler_params=_params("parallel"),
        name="outproj",
    )(h, gate, y_ret, y_s5_t, y_gla, wglu_t, bglu, w_rg, w_s)


def kernel(x, c, w_ada, b_ada, g_ffn1, ffn1_w1, ffn1_w3, ffn1_w2, g_mix, w_in, s5_lam_re, s5_lam_im, s5_log_dt, s5_b_re, s5_b_im, s5_c_re, s5_c_im, s5_d, s5_w_glu, s5_b_glu, gla_w_gate, gla_b_gate, w_out, g_ffn2, ffn2_w1, ffn2_w3, ffn2_w2, g_final):
    bsz, seq_len, d = x.shape
    depth = w_ada.shape[0]
    mod = _ada_mod(c, w_ada, b_ada).reshape(depth, bsz, N_MOD, 1, d)
    tables = _ret_tables(seq_len)
    ret_w = RET_HEADS * (2 * RET_DK + 2 * RET_DV)
    s5_w = s5_d.shape[1]
    h = x.reshape(bsz * seq_len, d)
    ffn1_w1, ffn1_w3, ffn1_w2, ffn2_w1, ffn2_w3, ffn2_w2 = (
        w.astype(BF16) for w in (ffn1_w1, ffn1_w3, ffn1_w2, ffn2_w1, ffn2_w3, ffn2_w2))
    w_in = w_in.astype(BF16)
    w_tok = jnp.concatenate([w_in[:, :, :ret_w], w_in[:, :, ret_w + s5_w:]], axis=2)
    w_tok = jnp.pad(w_tok, ((0, 0), (0, 0), (0, -w_tok.shape[2] % LANES)))
    w_s5_t = jnp.swapaxes(w_in[:, :, ret_w:ret_w + s5_w], 1, 2)
    w_out = w_out.astype(BF16)
    ret_o = RET_HEADS * RET_DV
    w_rg = jnp.concatenate([w_out[:, :ret_o], w_out[:, ret_o + s5_w:]], axis=1)
    w_s = w_out[:, ret_o:ret_o + s5_w]
    w_glu_t = jnp.swapaxes(s5_w_glu.astype(BF16), 1, 2)
    for l in range(depth):
        sh1, sc1, gt1, sh2, sc2, gt2, sh3, sc3, gt3 = (mod[l, :, i] for i in range(N_MOD))
        h = _ffn(h, seq_len, g_ffn1[l], sh1, sc1, gt1, ffn1_w1, ffn1_w3, ffn1_w2, l)
        proj, su_t = _inproj(h, seq_len, g_mix[l], sh2, sc2, w_tok, w_s5_t, l)
        proj = proj.reshape(bsz, seq_len, -1)
        y_ret = _retention(proj, tables)
        y_s5_t = _s5(su_t, seq_len, s5_lam_re[l], s5_lam_im[l], s5_log_dt[l],
                     s5_b_re[l], s5_b_im[l], s5_c_re[l], s5_c_im[l], s5_d[l])
        y_gla = _gla(proj, gla_w_gate[l], gla_b_gate[l], ret_w)
        h = _outproj(h, seq_len, gt2, y_ret.reshape(bsz * seq_len, -1), y_s5_t,
                     y_gla.reshape(bsz * seq_len, -1), w_glu_t, s5_b_glu, w_rg, w_s, l)
        h = _ffn(h, seq_len, g_ffn2[l], sh3, sc3, gt3, ffn2_w1, ffn2_w3, ffn2_w2, l,
                 g_final if l == depth - 1 else None)
    return h.reshape(bsz, seq_len, d)
```

```python
import functools
import math

import jax
import jax.numpy as jnp
from jax import lax
from jax.experimental import pallas as pl
from jax.experimental.pallas import tpu as pltpu

F32 = jnp.float32
BF16 = jnp.bfloat16
HIGHEST = lax.Precision.HIGHEST

RET_HEADS = 4
RET_DK = 128
RET_DV = 256
RET_CHUNK = 256
ROPE_BASE = 10000.0
S5_GROUP = 16
S5_STATE = 64
S5_CHUNK = 32
GLA_HEADS = 4
GLA_DK = 64
GLA_DV = 128
GLA_RANK = 16
GLA_TAU = 16.0
GLA_CHUNK = 64
GLA_BLOCK = 256
FFN_RES = 0.5
FFN_ROWS = 512
N_MOD = 9
EPS = 1e-6

LANES = 128
SUBLANES = 8
MXU_N = 256
VMEM_LIMIT = 60 << 20

NT_DIMS = (((1,), (1,)), ((), ()))
TN_DIMS = (((0,), (0,)), ((), ()))


def _params(*semantics):
    return pltpu.CompilerParams(dimension_semantics=semantics, vmem_limit_bytes=VMEM_LIMIT)


def _sigmoid(x):
    return 1.0 / (1.0 + jnp.exp(-x))


def _norm_mod(x, g, shift, scale):
    ms = jnp.mean(x * x, axis=-1, keepdims=True)
    return x * lax.rsqrt(ms + EPS) * (g * (1.0 + scale)) + shift


def _ada_kernel(c_ref, w_ref, b_ref, o_ref):
    c = c_ref[...]
    cond = c * _sigmoid(c)
    rows = cond.shape[0]
    c_hi = cond.astype(BF16).astype(F32)
    lhs = jnp.concatenate([c_hi, cond - c_hi], axis=0).astype(BF16)
    w = w_ref[0]
    w_hi = w.astype(BF16)
    w_mid = (w - w_hi.astype(F32)).astype(BF16)
    s = (jnp.dot(lhs, w_hi, preferred_element_type=F32)
         + jnp.dot(lhs, w_mid, preferred_element_type=F32))
    o_ref[0] = s[:rows] + s[rows:] + b_ref[0]


def _ada_mod(c, w_ada, b_ada):
    depth, d, n = w_ada.shape
    bsz = c.shape[0]
    rows = -(-bsz // 8) * 8
    c_pad = jnp.zeros((rows, d), F32).at[:bsz].set(c)
    tn = n // 16
    out = pl.pallas_call(
        _ada_kernel,
        grid=(depth, n // tn),
        in_specs=[
            pl.BlockSpec((rows, d), lambda l, j: (0, 0)),
            pl.BlockSpec((1, d, tn), lambda l, j: (l, 0, j)),
            pl.BlockSpec((1, 1, tn), lambda l, j: (l, 0, j)),
        ],
        out_specs=pl.BlockSpec((1, rows, tn), lambda l, j: (l, 0, j)),
        out_shape=jax.ShapeDtypeStruct((depth, rows, n), F32),
        compiler_params=_params("parallel", "parallel"),
        name="ada_mod",
    )(c_pad, w_ada, b_ada.reshape(depth, 1, n))
    return out[:, :bsz]


def _ffn_kernel(h_ref, g_ref, sh_ref, sc_ref, gt_ref, w1_ref, w3_ref, w2_ref, *rest, tf, nj, final):
    if final:
        gf_ref, o_ref, u_scr = rest
    else:
        o_ref, u_scr = rest
    j = pl.program_id(1)

    row_blocks = [slice(r0, r0 + FFN_ROWS) for r0 in range(0, o_ref.shape[0], FFN_ROWS)]

    def normalize(rows):
        u_scr[rows, :] = _norm_mod(h_ref[rows, :], g_ref[...], sh_ref[0], sc_ref[0]).astype(BF16)

    def matmuls(rows, first):
        u = u_scr[rows, :]
        acts = []
        for c0 in range(0, tf, MXU_N):
            a = jnp.dot(u, w1_ref[:, c0:c0 + MXU_N], preferred_element_type=F32)
            b = jnp.dot(u, w3_ref[:, c0:c0 + MXU_N], preferred_element_type=F32)
            acts.append((a * _sigmoid(a) * b).astype(BF16))
        down = jnp.dot(jnp.concatenate(acts, axis=1), w2_ref[...], preferred_element_type=F32)
        o_ref[rows, :] = down if first else o_ref[rows, :] + down

    def finish(rows):
        out = h_ref[rows, :] + (FFN_RES * gt_ref[0]) * o_ref[rows, :]
        if final:
            ms = jnp.mean(out * out, axis=-1, keepdims=True)
            out = out * lax.rsqrt(ms + EPS) * gf_ref[...]
        o_ref[rows, :] = out

    @pl.when(j == 0)
    def _():
        for rows in row_blocks:
            normalize(rows)
            matmuls(rows, True)

    @pl.when(jnp.logical_and(j > 0, j < nj - 1))
    def _():
        for rows in row_blocks:
            matmuls(rows, False)

    @pl.when(j == nj - 1)
    def _():
        for rows in row_blocks:
            matmuls(rows, False)
            finish(rows)


def _ffn(h, seq_len, g, shift, scale, gate, w1, w3, w2, layer, g_final=None, *, tm=1024, tf=512):
    m, d = h.shape
    dff = w1.shape[2]
    tm = min(tm, seq_len)
    nj = dff // tf
    assert nj >= 2 and tm % FFN_ROWS == 0
    per_batch = seq_len // tm
    final = g_final is not None
    vec = lambda i, j: (i // per_batch, 0, 0)
    in_specs = [
        pl.BlockSpec((tm, d), lambda i, j: (i, 0)),
        pl.BlockSpec((1, d), lambda i, j: (0, 0)),
        pl.BlockSpec((1, 1, d), vec),
        pl.BlockSpec((1, 1, d), vec),
        pl.BlockSpec((1, 1, d), vec),
        pl.BlockSpec((None, d, tf), lambda i, j: (layer, 0, j)),
        pl.BlockSpec((None, d, tf), lambda i, j: (layer, 0, j)),
        pl.BlockSpec((None, tf, d), lambda i, j: (layer, j, 0)),
    ]
    args = [h, g.reshape(1, d), shift, scale, gate, w1, w3, w2]
    if final:
        in_specs.append(pl.BlockSpec((1, d), lambda i, j: (0, 0)))
        args.append(g_final.reshape(1, d))
    return pl.pallas_call(
        functools.partial(_ffn_kernel, tf=tf, nj=nj, final=final),
        grid=(m // tm, nj),
        in_specs=in_specs,
        out_specs=pl.BlockSpec((tm, d), lambda i, j: (i, 0)),
        out_shape=jax.ShapeDtypeStruct((m, d), F32),
        scratch_shapes=[pltpu.VMEM((tm, d), BF16)],
        compiler_params=_params("parallel", "arbitrary"),
        name="ffn",
    )(*args)


def _inproj_kernel(h_ref, g_ref, sh_ref, sc_ref, w_ref, wst_ref, o_ref, ost_ref, *, steps, s5_cols):
    u = _norm_mod(h_ref[...], g_ref[...], sh_ref[0], sc_ref[0]).astype(BF16)
    lo, hi = s5_cols
    o_ref[:, :lo] = jnp.dot(u, w_ref[:, :lo], preferred_element_type=F32)
    o_ref[:, lo:] = jnp.dot(u, w_ref[:, hi:], preferred_element_type=F32)
    su_t = lax.dot_general(wst_ref[...], u, NT_DIMS, preferred_element_type=F32)
    sub = su_t.shape[1] // LANES
    base = (pl.program_id(0) % steps) * sub
    for k in range(sub):
        ost_ref[:, pl.ds(base + k, 1), :] = su_t[:, k * LANES:(k + 1) * LANES][:, None, :]


def _inproj(h, seq_len, g, shift, scale, w, w_s5_t, layer, s5_cols, *, tm=256):
    m, d = h.shape
    w_s5 = w_s5_t.shape[1]
    d_pad = w.shape[2]
    d_out = d_pad - (s5_cols[1] - s5_cols[0])
    tm = min(tm, seq_len)
    per_batch = seq_len // tm
    steps = SUBLANES * LANES // tm
    vec = lambda i: (i // per_batch, 0, 0)
    return pl.pallas_call(
        functools.partial(_inproj_kernel, steps=steps, s5_cols=s5_cols),
        grid=(m // tm,),
        in_specs=[
            pl.BlockSpec((tm, d), lambda i: (i, 0)),
            pl.BlockSpec((1, d), lambda i: (0, 0)),
            pl.BlockSpec((1, 1, d), vec),
            pl.BlockSpec((1, 1, d), vec),
            pl.BlockSpec((None, d, d_pad), lambda i: (layer, 0, 0), pipeline_mode=pl.Buffered(1)),
            pl.BlockSpec((None, w_s5, d), lambda i: (layer, 0, 0), pipeline_mode=pl.Buffered(1)),
        ],
        out_specs=[pl.BlockSpec((tm, d_out), lambda i: (i, 0)),
                   pl.BlockSpec((w_s5, SUBLANES, LANES), lambda i: (0, i // steps, 0))],
        out_shape=[jax.ShapeDtypeStruct((m, d_out), F32),
                   jax.ShapeDtypeStruct((w_s5, m // LANES, LANES), F32)],
        compiler_params=_params("arbitrary"),
        name="inproj",
    )(h, g.reshape(1, d), shift, scale, w, w_s5_t)


def _ret_tables(seq_len):
    c = RET_CHUNK
    pos = jnp.arange(seq_len, dtype=F32)
    inv_freq = ROPE_BASE ** (-jnp.arange(0, RET_DK, 2, dtype=F32) / RET_DK)
    ang = pos[:, None] * inv_freq[None, :]
    cos, sin = jnp.cos(ang), jnp.sin(ang)
    cs = jnp.concatenate([cos, cos], axis=-1)
    sn = jnp.concatenate([-sin, sin], axis=-1)
    lg = jnp.log1p(-jnp.exp2(-5.0 - jnp.arange(RET_HEADS, dtype=F32)))
    idx = jnp.arange(c, dtype=F32)
    dist = jnp.abs(idx[:, None] - idx[None, :])
    dmat = jnp.exp(dist[None] * lg[:, None, None])
    rows = jnp.stack([c - 1.0 - idx, idx, idx + 1.0, c - idx])
    dvec = jnp.exp(rows[None] * lg[:, None, None])
    dvec = jnp.broadcast_to(dvec[..., None], dvec.shape + (RET_DK,))
    gc = jnp.broadcast_to(jnp.exp(c * lg)[:, None, None], (RET_HEADS, 1, RET_DV))
    return cs, sn, dmat, dvec, gc


def _ret_kernel(q_ref, k_ref, v_ref, g_ref, cs_ref, sn_ref, dmat_ref, dvec_ref, gc_ref, o_ref,
                q_scr, k_scr, qf_scr, qb_scr, kf_scr, kb_scr, of_scr, ob_scr, sf_scr, sb_scr,
                *, nchunks):
    c = RET_CHUNK
    scale = RET_DK ** -0.5
    chunk_rows = lambda n: pl.ds(pl.multiple_of(n * c, c), c)

    def prepare(n, carry):
        rows = chunk_rows(n)
        cs = cs_ref[rows, :]
        sn = sn_ref[rows, :]
        q = q_ref[0, rows, :]
        k = k_ref[0, rows, :]
        q = (q * cs + pltpu.roll(q, RET_DK // 2, 1) * sn) * scale
        k = k * cs + pltpu.roll(k, RET_DK // 2, 1) * sn
        q_scr[rows, :] = q.astype(BF16)
        k_scr[rows, :] = k.astype(BF16)
        kf_scr[rows, :] = (k * dvec_ref[0, 0]).astype(BF16)
        kb_scr[rows, :] = (k * dvec_ref[0, 1]).astype(BF16)
        qf_scr[rows, :] = (q * dvec_ref[0, 2]).astype(BF16)
        qb_scr[rows, :] = (q * dvec_ref[0, 3]).astype(BF16)
        return carry

    lax.fori_loop(0, nchunks, prepare, 0, unroll=4)
    sf_scr[...] = jnp.zeros_like(sf_scr)
    sb_scr[...] = jnp.zeros_like(sb_scr)
    gc = gc_ref[0]

    def recur(n, carry):
        rf = chunk_rows(n)
        vf = v_ref[0, rf, :].astype(BF16)
        scores = lax.dot_general(q_scr[rf, :], k_scr[rf, :], NT_DIMS,
                                 preferred_element_type=F32) * dmat_ref[0]
        s = sf_scr[...]
        of_scr[rf, :] = (jnp.dot(scores.astype(BF16), vf, preferred_element_type=F32)
                         + jnp.dot(qf_scr[rf, :], s.astype(BF16), preferred_element_type=F32))
        sf_scr[...] = gc * s + lax.dot_general(kf_scr[rf, :], vf, TN_DIMS,
                                               preferred_element_type=F32)
        rb = chunk_rows(nchunks - 1 - n)
        vb = v_ref[0, rb, :].astype(BF16)
        s = sb_scr[...]
        ob_scr[rb, :] = jnp.dot(qb_scr[rb, :], s.astype(BF16), preferred_element_type=F32)
        sb_scr[...] = gc * s + lax.dot_general(kb_scr[rb, :], vb, TN_DIMS,
                                               preferred_element_type=F32)
        return carry

    lax.fori_loop(0, nchunks, recur, 0, unroll=4)

    def finish(n, carry):
        rows = chunk_rows(n)
        o = of_scr[rows, :] + ob_scr[rows, :]
        oc = o - jnp.mean(o, axis=-1, keepdims=True)
        ln = oc * lax.rsqrt(jnp.mean(oc * oc, axis=-1, keepdims=True) + EPS)
        g = g_ref[0, rows, :]
        o_ref[0, rows, :] = (g * _sigmoid(g) * ln).astype(o_ref.dtype)
        return carry

    lax.fori_loop(0, nchunks, finish, 0, unroll=4)


def _retention(proj, tables):
    bsz, seq_len, _ = proj.shape
    cs, sn, dmat, dvec, gc = tables
    h = RET_HEADS
    kq, kk = 0, h
    kv, kg = (2 * h * RET_DK) // RET_DV, (2 * h * RET_DK + h * RET_DV) // RET_DV
    tok = lambda off: (lambda b, i: (b, 0, off + i))
    return pl.pallas_call(
        functools.partial(_ret_kernel, nchunks=seq_len // RET_CHUNK),
        grid=(bsz, h),
        in_specs=[
            pl.BlockSpec((1, seq_len, RET_DK), tok(kq)),
            pl.BlockSpec((1, seq_len, RET_DK), tok(kk)),
            pl.BlockSpec((1, seq_len, RET_DV), tok(kv)),
            pl.BlockSpec((1, seq_len, RET_DV), tok(kg)),
            pl.BlockSpec((seq_len, RET_DK), lambda b, i: (0, 0), pipeline_mode=pl.Buffered(1)),
            pl.BlockSpec((seq_len, RET_DK), lambda b, i: (0, 0), pipeline_mode=pl.Buffered(1)),
            pl.BlockSpec((1, RET_CHUNK, RET_CHUNK), lambda b, i: (i, 0, 0)),
            pl.BlockSpec((1, 4, RET_CHUNK, RET_DK), lambda b, i: (i, 0, 0, 0)),
            pl.BlockSpec((1, 1, RET_DV), lambda b, i: (i, 0, 0)),
        ],
        out_specs=pl.BlockSpec((1, seq_len, RET_DV), lambda b, i: (b, 0, i)),
        out_shape=jax.ShapeDtypeStruct((bsz, seq_len, h * RET_DV), BF16),
        scratch_shapes=(
            [pltpu.VMEM((seq_len, RET_DK), BF16)] * 6
            + [pltpu.VMEM((seq_len, RET_DV), F32)] * 2
            + [pltpu.VMEM((RET_DK, RET_DV), F32)] * 2
        ),
        compiler_params=_params("parallel", "parallel"),
        name="retention",
    )(proj, proj, proj, proj, cs, sn, dmat, dvec, gc)


def _gla_kernel(q_ref, k_ref, v_ref, g_ref, lr_ref, wf_ref, wb_ref, bf_ref, bb_ref, o_ref,
                q0_scr, q1_scr, kin_scr, kst_scr, dec_scr, o_scr, st_scr, *, nblocks):
    c, r = GLA_CHUNK, GLA_BLOCK
    per = r // c
    dk2, dv2 = 2 * GLA_DK, 2 * GLA_DV
    scale = GLA_DK ** -0.5
    block_rows = lambda m: pl.ds(pl.multiple_of(m * r, r), r)

    ri = lax.broadcasted_iota(jnp.int32, (r, r), 0)
    ci = lax.broadcasted_iota(jnp.int32, (r, r), 1)
    same_chunk = (ri // c) == (ci // c)
    lane = lax.broadcasted_iota(jnp.int32, (r, dk2), 1)
    v_lane = lax.broadcasted_iota(jnp.int32, (r, dv2), 1)
    se = lax.broadcasted_iota(jnp.int32, (dv2, dk2), 0)
    sd = lax.broadcasted_iota(jnp.int32, (dv2, dk2), 1)
    same_head = jnp.where(se < GLA_DV, jnp.where(sd < GLA_DK, 1.0, 0.0),
                          jnp.where(sd >= GLA_DK, 1.0, 0.0)).astype(F32)

    def direction(w_ref, b_ref, reverse):
        if reverse:
            cum = same_chunk & (ci >= ri)
            keep = same_chunk & (ci > ri)
        else:
            cum = same_chunk & (ci <= ri)
            keep = same_chunk & (ci <= ri)
        sum_mat = jnp.concatenate([jnp.where(cum, 1.0, 0.0), jnp.where(same_chunk, 1.0, 0.0)],
                                  axis=0).astype(BF16)

        def prepare(m, carry):
            rows = block_rows(m)
            lr = lr_ref[0, rows, :]
            lr_hi = lr.astype(BF16)
            lr_mid = (lr - lr_hi.astype(F32)).astype(BF16)
            x = jnp.dot(jnp.concatenate([lr_hi, lr_mid, lr_hi], axis=1), w_ref[0],
                        preferred_element_type=F32) + b_ref[0]
            la = (jnp.minimum(x, 0.0) - jnp.log(1.0 + jnp.exp(-jnp.abs(x)))) * (1.0 / GLA_TAU)
            hi = la.astype(BF16)
            rest = la - hi.astype(F32)
            mid = rest.astype(BF16)
            lo = (rest - mid.astype(F32)).astype(BF16)
            sums = jnp.dot(sum_mat, jnp.concatenate([hi, mid, lo], axis=1),
                           preferred_element_type=F32)
            sums = sums[:, :dk2] + sums[:, dk2:2 * dk2] + sums[:, 2 * dk2:]
            b, total = sums[:r], sums[r:]
            q_in = q_ref[0, rows, :] * scale * jnp.exp(b)
            k = k_ref[0, rows, :]
            q0_scr[rows, :] = jnp.where(lane < GLA_DK, q_in, 0.0).astype(BF16)
            q1_scr[rows, :] = jnp.where(lane >= GLA_DK, q_in, 0.0).astype(BF16)
            kin_scr[rows, :] = (k * jnp.exp(-b)).astype(BF16)
            kst_scr[rows, :] = (k * jnp.exp(total - b)).astype(BF16)
            dec_scr[rows, :] = jnp.exp(total)
            return carry

        lax.fori_loop(0, nblocks, prepare, 0, unroll=4)
        st_scr[...] = jnp.zeros_like(st_scr)

        def recur(t, carry):
            rows = block_rows(nblocks - 1 - t if reverse else t)
            q0 = q0_scr[rows, :]
            q1 = q1_scr[rows, :]
            kin = kin_scr[rows, :]
            kst = kst_scr[rows, :]
            dec = dec_scr[rows, :]
            vb = v_ref[0, rows, :].astype(BF16)
            scores = [jnp.where(keep, lax.dot_general(qh, kin, NT_DIMS, preferred_element_type=F32),
                                0.0).astype(BF16) for qh in (q0, q1)]
            v_heads = jnp.concatenate([jnp.where(v_lane < GLA_DV, vb, jnp.zeros_like(vb)),
                                       jnp.where(v_lane >= GLA_DV, vb, jnp.zeros_like(vb))], axis=0)
            intra = jnp.dot(jnp.concatenate(scores, axis=1), v_heads, preferred_element_type=F32)
            qin = q0 + q1
            st = st_scr[...]
            inter = [None] * per
            for i in (range(per - 1, -1, -1) if reverse else range(per)):
                sl = slice(i * c, (i + 1) * c)
                inter[i] = lax.dot_general(qin[sl], st.astype(BF16), NT_DIMS,
                                           preferred_element_type=F32)
                kv_t = lax.dot_general(vb[sl], kst[sl], TN_DIMS, preferred_element_type=F32)
                st = st * dec[i * c:i * c + 1, :] + kv_t * same_head
            st_scr[...] = st
            o = intra + jnp.concatenate(inter, axis=0)
            if not reverse:
                o_scr[rows, :] = o
            else:
                o = o + o_scr[rows, :]
                normed = []
                for hh in range(2):
                    oh = o[:, hh * GLA_DV:(hh + 1) * GLA_DV]
                    normed.append(oh * lax.rsqrt(jnp.mean(oh * oh, axis=-1, keepdims=True) + EPS))
                g = g_ref[0, rows, :]
                o_ref[0, rows, :] = (g * _sigmoid(g)
                                     * jnp.concatenate(normed, axis=1)).astype(o_ref.dtype)
            return carry

        lax.fori_loop(0, nblocks, recur, 0, unroll=4)

    direction(wf_ref, bf_ref, False)
    direction(wb_ref, bb_ref, True)


def _gla(proj, w_gate, b_gate, col0):
    bsz, seq_len, _ = proj.shape
    pairs = GLA_HEADS // 2
    dk2, dv2 = 2 * GLA_DK, 2 * GLA_DV
    c_q = col0
    c_k = c_q + GLA_HEADS * GLA_DK
    c_v = c_k + GLA_HEADS * GLA_DK
    c_g = c_v + GLA_HEADS * GLA_DV
    c_lr = c_g + GLA_HEADS * GLA_DV
    wg = w_gate.astype(F32).reshape(2, GLA_RANK, pairs, dk2).transpose(0, 2, 1, 3)
    wf = jnp.zeros((pairs, LANES, dk2), F32).at[:, :GLA_RANK].set(wg[0])
    wb = jnp.zeros((pairs, LANES, dk2), F32).at[:, GLA_RANK:2 * GLA_RANK].set(wg[1])

    def split3(w):
        w_hi = w.astype(BF16)
        w_mid = (w - w_hi.astype(F32)).astype(BF16)
        return jnp.concatenate([w_hi, w_hi, w_mid], axis=1)

    wf, wb = split3(wf), split3(wb)
    bg = b_gate.astype(F32).reshape(2, pairs, 1, dk2)
    tok = lambda col, width: (lambda b, p: (b, 0, col // width + p))
    mat = lambda b, p: (p, 0, 0)
    return pl.pallas_call(
        functools.partial(_gla_kernel, nblocks=seq_len // GLA_BLOCK),
        grid=(bsz, pairs),
        in_specs=[
            pl.BlockSpec((1, seq_len, dk2), tok(c_q, dk2)),
            pl.BlockSpec((1, seq_len, dk2), tok(c_k, dk2)),
            pl.BlockSpec((1, seq_len, dv2), tok(c_v, dv2)),
            pl.BlockSpec((1, seq_len, dv2), tok(c_g, dv2)),
            pl.BlockSpec((1, seq_len, LANES), lambda b, p: (b, 0, c_lr // LANES)),
            pl.BlockSpec((1, 3 * LANES, dk2), mat),
            pl.BlockSpec((1, 3 * LANES, dk2), mat),
            pl.BlockSpec((1, 1, dk2), mat),
            pl.BlockSpec((1, 1, dk2), mat),
        ],
        out_specs=pl.BlockSpec((1, seq_len, dv2), lambda b, p: (b, 0, p)),
        out_shape=jax.ShapeDtypeStruct((bsz, seq_len, GLA_HEADS * GLA_DV), BF16),
        scratch_shapes=(
            [pltpu.VMEM((seq_len, dk2), BF16)] * 4
            + [pltpu.VMEM((seq_len, dk2), F32), pltpu.VMEM((seq_len, dv2), F32),
               pltpu.VMEM((dv2, dk2), F32)]
        ),
        compiler_params=_params("parallel", "parallel"),
        name="gla",
    )(proj, proj, proj, proj, proj, wf, wb, bg[0], bg[1])


def _swap(x):
    return pltpu.roll(x, S5_STATE, 1)


def _s5_prep_kernel(lre_ref, lim_ref, ldt_ref, brr_ref, bis_ref, cr_ref, ci_ref,
                    m_ref, e_ref, ft_ref, a_ref):
    t_len = S5_CHUNK
    gsz = S5_GROUP
    width = gsz * t_len
    first = lax.broadcasted_iota(jnp.int32, (1, LANES), 1) < S5_STATE
    one_zero = jnp.where(first, 1.0, 0.0).astype(F32)
    sign = jnp.where(first, 1.0, -1.0).astype(F32)

    def packed_exp(t, are, aim):
        ph = t * aim
        return jnp.exp(t * are) * jnp.where(first, jnp.cos(ph), jnp.sin(ph))

    def dup_re(x):
        return jnp.where(first, x, _swap(x))

    def dup_im_signed(x):
        return jnp.where(first, -_swap(x), x)

    def per_channel(pw, w_a, w_b):
        return jnp.concatenate([pw * w_a[c:c + 1, :] + _swap(pw) * w_b[c:c + 1, :]
                                for c in range(gsz)], axis=0)

    trow = lax.broadcasted_iota(jnp.int32, (t_len, 1), 0).astype(F32)
    panels = []
    for d in range(2):
        lre = lre_ref[0, d:d + 1, :]
        lim = lim_ref[0, d:d + 1, :]
        dt = jnp.exp(ldt_ref[0, d:d + 1, :])
        are, aim = lre * dt, lim * dt
        num = packed_exp(jnp.ones((1, 1), F32), are, aim) - one_zero
        inv_den = 1.0 / (lre * lre + lim * lim)
        coef = num * (lre * inv_den) + _swap(num) * (lim * inv_den * sign)
        bbar = coef * brr_ref[0, d] + _swap(coef) * bis_ref[0, d]
        if d == 0:
            t_e, t_f, t_g = t_len - 1.0 - trow, trow + 1.0, trow
        else:
            t_e, t_f, t_g = trow, t_len - trow, t_len - 1.0 - trow
        e_mat = per_channel(packed_exp(t_e, are, aim), dup_re(bbar), dup_im_signed(bbar))
        e_ref[0, :, d * LANES:(d + 1) * LANES] = e_mat.astype(BF16)
        cr, ci = cr_ref[0, d], ci_ref[0, d]
        ft_ref[0, d] = per_channel(packed_exp(t_f, are, aim), cr, ci).astype(BF16)
        g_mat = per_channel(packed_exp(t_g, are, aim), cr, ci)
        panels.append(lax.dot_general(bbar, g_mat, NT_DIMS, preferred_element_type=F32,
                                      precision=HIGHEST))
        kcol = lax.broadcasted_iota(jnp.int32, (8, 1), 0)
        tk = (t_len * jnp.left_shift(1, kcol)).astype(F32)
        ak = packed_exp(tk, are, aim)
        a_ref[0, d, 0:8, :] = dup_re(ak)
        a_ref[0, d, 8:16, :] = dup_im_signed(ak)

    p_fwd, p_bwd = panels
    j = lax.broadcasted_iota(jnp.int32, (t_len, width), 0)
    pos = lax.broadcasted_iota(jnp.int32, (t_len, width), 1) % t_len
    for c in range(gsz):
        fwd = pltpu.roll(jnp.broadcast_to(p_fwd[c:c + 1, :], (t_len, width)), 0, 1,
                         stride=1, stride_axis=0)
        bwd = pltpu.roll(jnp.broadcast_to(p_bwd[c:c + 1, :], (t_len, width)), width - (t_len - 1), 1,
                         stride=1, stride_axis=0)
        blk = jnp.where(pos >= j, fwd, 0.0) + jnp.where(pos <= j, bwd, 0.0)
        m_ref[0, c * t_len:(c + 1) * t_len, :] = blk.astype(BF16)


def _s5_apply_kernel(u_ref, m_ref, e_ref, ft_ref, a_ref, d_ref, y_ref, *, blocks_per_seq, nsteps):
    t_len = S5_CHUNK
    per = LANES // t_len
    nblk = u_ref.shape[1]
    xc = [u_ref[c] for c in range(S5_GROUP)]
    ys, sums = [], []
    for q in range(per):
        u = jnp.concatenate([x[:, q * t_len:(q + 1) * t_len] for x in xc], axis=1)
        ub = u.astype(BF16)
        ys.append(jnp.dot(ub, m_ref[0], preferred_element_type=F32) + u * d_ref[0])
        sums.append(jnp.dot(ub, e_ref[0], preferred_element_type=F32))
    nidx = lax.broadcasted_iota(jnp.int32, (nblk, LANES), 0) % blocks_per_seq
    for d in range(2):

        def block_shift(val, sh):
            if d == 0:
                return jnp.where(nidx >= sh, pltpu.roll(val, sh, 0), 0.0)
            return jnp.where(nidx < blocks_per_seq - sh, pltpu.roll(val, nblk - sh, 0), 0.0)

        def chunk_shift(x, sh):
            if sh % per == 0:
                return [block_shift(v, sh // per) for v in x]
            src = [q - sh if d == 0 else q + sh for q in range(per)]
            return [x[s] if 0 <= s < per else block_shift(x[s % per], 1) for s in src]

        x = [s[:, d * LANES:(d + 1) * LANES] for s in sums]
        for k in range(nsteps):
            xs = chunk_shift(x, 1 << k)
            x = [v + a_ref[0, d, k:k + 1, :] * w + a_ref[0, d, 8 + k:9 + k, :] * _swap(w)
                 for v, w in zip(x, xs)]
        for q, v in enumerate(chunk_shift(x, 1)):
            ys[q] = ys[q] + lax.dot_general(v.astype(BF16), ft_ref[0, d], NT_DIMS,
                                            preferred_element_type=F32)
    for c in range(S5_GROUP):
        y_ref[c] = jnp.concatenate([y[:, c * t_len:(c + 1) * t_len] for y in ys], axis=1)


def _s5(su_t, seq_len, lam_re, lam_im, log_dt, b_re, b_im, c_re, c_im, d_skip):
    w, nblk, _ = su_t.shape
    g = w // S5_GROUP
    t_len = S5_CHUNK
    nchunks = seq_len // t_len
    nsteps = max(1, (nchunks - 1).bit_length())
    assert nsteps <= 8 and seq_len % LANES == 0
    width = S5_GROUP * t_len
    dup = lambda a: jnp.concatenate([a, a], axis=-1)
    gd = lambda a: jnp.moveaxis(a.astype(F32), 0, 1)
    lre = dup(gd(lam_re))
    lim = dup(gd(lam_im))
    ldt = jnp.broadcast_to(gd(log_dt)[..., None], (g, 2, LANES))
    bt_re = jnp.swapaxes(gd(b_re), -1, -2)
    bt_im = jnp.swapaxes(gd(b_im), -1, -2)
    brr = jnp.concatenate([bt_re, bt_re], axis=-1)
    bis = jnp.concatenate([-bt_im, bt_im], axis=-1)
    cre, cim = gd(c_re), gd(c_im)
    cr = jnp.concatenate([cre, -cre], axis=-1)
    ci = jnp.concatenate([-cim, -cim], axis=-1)
    vec3 = pl.BlockSpec((1, 2, LANES), lambda i: (i, 0, 0))
    mat4 = pl.BlockSpec((1, 2, S5_GROUP, LANES), lambda i: (i, 0, 0, 0))
    m_mat, e_mat, ft_mat, a_mat = pl.pallas_call(
        _s5_prep_kernel,
        grid=(g,),
        in_specs=[vec3, vec3, vec3, mat4, mat4, mat4, mat4],
        out_specs=[
            pl.BlockSpec((1, width, width), lambda i: (i, 0, 0)),
            pl.BlockSpec((1, width, 2 * LANES), lambda i: (i, 0, 0)),
            pl.BlockSpec((1, 2, width, LANES), lambda i: (i, 0, 0, 0)),
            pl.BlockSpec((1, 2, 16, LANES), lambda i: (i, 0, 0, 0)),
        ],
        out_shape=[
            jax.ShapeDtypeStruct((g, width, width), BF16),
            jax.ShapeDtypeStruct((g, width, 2 * LANES), BF16),
            jax.ShapeDtypeStruct((g, 2, width, LANES), BF16),
            jax.ShapeDtypeStruct((g, 2, 16, LANES), F32),
        ],
        compiler_params=_params("parallel"),
        name="s5_prep",
    )(lre, lim, ldt, brr, bis, cr, ci)
    d_rep = jnp.repeat(d_skip.astype(F32).reshape(g, 1, S5_GROUP), t_len, axis=-1)
    blk = pl.BlockSpec((S5_GROUP, nblk, LANES), lambda i: (i, 0, 0))
    return pl.pallas_call(
        functools.partial(_s5_apply_kernel, blocks_per_seq=seq_len // LANES, nsteps=nsteps),
        grid=(g,),
        in_specs=[
            blk,
            pl.BlockSpec((1, width, width), lambda i: (i, 0, 0)),
            pl.BlockSpec((1, width, 2 * LANES), lambda i: (i, 0, 0)),
            pl.BlockSpec((1, 2, width, LANES), lambda i: (i, 0, 0, 0)),
            pl.BlockSpec((1, 2, 16, LANES), lambda i: (i, 0, 0, 0)),
            pl.BlockSpec((1, 1, width), lambda i: (i, 0, 0)),
        ],
        out_specs=blk,
        out_shape=jax.ShapeDtypeStruct(su_t.shape, F32),
        compiler_params=_params("parallel"),
        name="s5_apply",
    )(su_t, m_mat, e_mat, ft_mat, a_mat, d_rep)


def _outproj_kernel(h_ref, gt_ref, yr_ref, ys_ref, yg_ref, wglu_ref, bglu_ref, wout_ref, o_ref,
                    *, steps):
    sub = o_ref.shape[0] // LANES
    base = (pl.program_id(0) % steps) * sub
    y = jnp.concatenate([ys_ref[:, pl.ds(base + k, 1), :][:, 0, :] for k in range(sub)], axis=1)
    z = y * (0.5 * (1.0 + jnp.tanh(math.sqrt(2.0 / math.pi) * (y + 0.044715 * (y * y * y)))))
    gl = jnp.dot(wglu_ref[...], z.astype(BF16), preferred_element_type=F32) + bglu_ref[...]
    s5 = (z * _sigmoid(gl)).astype(BF16)
    r_end = yr_ref.shape[1]
    s_end = r_end + s5.shape[0]
    acc = jnp.dot(yr_ref[...], wout_ref[:r_end, :], preferred_element_type=F32)
    acc = acc + jnp.dot(yg_ref[...], wout_ref[s_end:, :], preferred_element_type=F32)
    acc = acc + lax.dot_general(s5, wout_ref[r_end:s_end, :], TN_DIMS, preferred_element_type=F32)
    o_ref[...] = h_ref[...] + gt_ref[0] * acc


def _outproj(h, seq_len, gate, y_ret, y_s5_t, y_gla, wglu_t, b_glu, w_out, layer, *, tm=512):
    m, d = h.shape
    tm = min(tm, seq_len)
    per_batch = seq_len // tm
    steps = SUBLANES * LANES // tm
    w_s5 = y_s5_t.shape[0]
    row = lambda width: pl.BlockSpec((tm, width), lambda i: (i, 0))
    whole = lambda a: pl.BlockSpec((None,) + a.shape[1:], lambda i: (layer, 0, 0),
                                   pipeline_mode=pl.Buffered(1))
    bglu = b_glu.astype(F32).reshape(-1, w_s5, 1)
    return pl.pallas_call(
        functools.partial(_outproj_kernel, steps=steps),
        grid=(m // tm,),
        in_specs=[
            row(d),
            pl.BlockSpec((1, 1, d), lambda i: (i // per_batch, 0, 0)),
            row(y_ret.shape[1]),
            pl.BlockSpec((w_s5, SUBLANES, LANES), lambda i: (0, i // steps, 0)),
            row(y_gla.shape[1]),
            whole(wglu_t), whole(bglu), whole(w_out),
        ],
        out_specs=row(d),
        out_shape=jax.ShapeDtypeStruct((m, d), F32),
        compiler_params=_params("parallel"),
        name="outproj",
    )(h, gate, y_ret, y_s5_t, y_gla, wglu_t, bglu, w_out)


def kernel(x, c, w_ada, b_ada, g_ffn1, ffn1_w1, ffn1_w3, ffn1_w2, g_mix, w_in, s5_lam_re, s5_lam_im, s5_log_dt, s5_b_re, s5_b_im, s5_c_re, s5_c_im, s5_d, s5_w_glu, s5_b_glu, gla_w_gate, gla_b_gate, w_out, g_ffn2, ffn2_w1, ffn2_w3, ffn2_w2, g_final):
    bsz, seq_len, d = x.shape
    depth = w_ada.shape[0]
    mod = _ada_mod(c, w_ada, b_ada).reshape(depth, bsz, N_MOD, 1, d)
    tables = _ret_tables(seq_len)
    ret_w = RET_HEADS * (2 * RET_DK + 2 * RET_DV)
    s5_w = s5_d.shape[1]
    h = x.reshape(bsz * seq_len, d)
    ffn1_w1, ffn1_w3, ffn1_w2, ffn2_w1, ffn2_w3, ffn2_w2 = (
        w.astype(BF16) for w in (ffn1_w1, ffn1_w3, ffn1_w2, ffn2_w1, ffn2_w3, ffn2_w2))
    s5_cols = (ret_w, ret_w + s5_w)
    w_s5_t = jnp.swapaxes(w_in[:, :, s5_cols[0]:s5_cols[1]], 1, 2).astype(BF16)
    w_in = jnp.pad(w_in.astype(BF16), ((0, 0), (0, 0), (0, -w_in.shape[2] % LANES)))
    w_out = w_out.astype(BF16)
    w_glu_t = jnp.swapaxes(s5_w_glu.astype(BF16), 1, 2)
    for l in range(depth):
        sh1, sc1, gt1, sh2, sc2, gt2, sh3, sc3, gt3 = (mod[l, :, i] for i in range(N_MOD))
        h = _ffn(h, seq_len, g_ffn1[l], sh1, sc1, gt1, ffn1_w1, ffn1_w3, ffn1_w2, l)
        proj, su_t = _inproj(h, seq_len, g_mix[l], sh2, sc2, w_in, w_s5_t, l, s5_cols)
        proj = proj.reshape(bsz, seq_len, -1)
        y_ret = _retention(proj, tables)
        y_s5_t = _s5(su_t, seq_len, s5_lam_re[l], s5_lam_im[l], s5_log_dt[l],
                     s5_b_re[l], s5_b_im[l], s5_c_re[l], s5_c_im[l], s5_d[l])
        y_gla = _gla(proj, gla_w_gate[l], gla_b_gate[l], ret_w)
        h = _outproj(h, seq_len, gt2, y_ret.reshape(bsz * seq_len, -1), y_s5_t,
                     y_gla.reshape(bsz * seq_len, -1), w_glu_t, s5_b_glu, w_out, l)
        h = _ffn(h, seq_len, g_ffn2[l], sh3, sc3, gt3, ffn2_w1, ffn2_w3, ffn2_w2, l,
                 g_final if l == depth - 1 else None)
    return h.reshape(bsz, seq_len, d)
```

```python
import functools
import math

import jax
import jax.numpy as jnp
from jax import lax
from jax.experimental import pallas as pl
from jax.experimental.pallas import tpu as pltpu

F32 = jnp.float32
BF16 = jnp.bfloat16
HIGHEST = lax.Precision.HIGHEST

RET_HEADS = 4
RET_DK = 128
RET_DV = 256
RET_CHUNK = 256
ROPE_BASE = 10000.0
S5_GROUP = 16
S5_STATE = 64
S5_CHUNK = 32
GLA_HEADS = 4
GLA_DK = 64
GLA_DV = 128
GLA_RANK = 16
GLA_TAU = 16.0
GLA_CHUNK = 64
GLA_BLOCK = 256
FFN_RES = 0.5
FFN_ROWS = 512
N_MOD = 9
EPS = 1e-6

LANES = 128
SUBLANES = 8
MXU_N = 256
VMEM_LIMIT = 60 << 20

NT_DIMS = (((1,), (1,)), ((), ()))
TN_DIMS = (((0,), (0,)), ((), ()))


def _params(*semantics):
    return pltpu.CompilerParams(dimension_semantics=semantics, vmem_limit_bytes=VMEM_LIMIT)


def _sigmoid(x):
    return 1.0 / (1.0 + jnp.exp(-x))


def _norm_mod(x, g, shift, scale):
    ms = jnp.mean(x * x, axis=-1, keepdims=True)
    return x * lax.rsqrt(ms + EPS) * (g * (1.0 + scale)) + shift


def _ada_kernel(c_ref, w_ref, b_ref, o_ref):
    c = c_ref[...]
    cond = c * _sigmoid(c)
    rows = cond.shape[0]
    c_hi = cond.astype(BF16).astype(F32)
    lhs = jnp.concatenate([c_hi, cond - c_hi], axis=0).astype(BF16)
    w = w_ref[0]
    w_hi = w.astype(BF16)
    w_mid = (w - w_hi.astype(F32)).astype(BF16)
    s = (jnp.dot(lhs, w_hi, preferred_element_type=F32)
         + jnp.dot(lhs, w_mid, preferred_element_type=F32))
    o_ref[0] = s[:rows] + s[rows:] + b_ref[0]


def _ada_mod(c, w_ada, b_ada):
    depth, d, n = w_ada.shape
    bsz = c.shape[0]
    rows = -(-bsz // 8) * 8
    c_pad = jnp.zeros((rows, d), F32).at[:bsz].set(c)
    tn = n // 8
    out = pl.pallas_call(
        _ada_kernel,
        grid=(depth, n // tn),
        in_specs=[
            pl.BlockSpec((rows, d), lambda l, j: (0, 0)),
            pl.BlockSpec((1, d, tn), lambda l, j: (l, 0, j)),
            pl.BlockSpec((1, 1, tn), lambda l, j: (l, 0, j)),
        ],
        out_specs=pl.BlockSpec((1, rows, tn), lambda l, j: (l, 0, j)),
        out_shape=jax.ShapeDtypeStruct((depth, rows, n), F32),
        compiler_params=_params("parallel", "parallel"),
        name="ada_mod",
    )(c_pad, w_ada, b_ada.reshape(depth, 1, n))
    return out[:, :bsz]


def _ffn_kernel(h_ref, g_ref, sh_ref, sc_ref, gt_ref, w1_ref, w3_ref, w2_ref, *rest, tf, nj, final,
                cast_next):
    rest = list(rest)
    next_f32 = [rest.pop(0) for _ in range(3)] if cast_next else []
    gf_ref = rest.pop(0) if final else None
    o_ref = rest.pop(0)
    next_bf16 = [rest.pop(0) for _ in range(3)] if cast_next else []
    (u_scr,) = rest
    j = pl.program_id(1)

    def cast_next_weights():
        for src, dst in zip(next_f32, next_bf16):
            dst[...] = src[...].astype(BF16)

    row_blocks = [slice(r0, r0 + FFN_ROWS) for r0 in range(0, o_ref.shape[0], FFN_ROWS)]

    def normalize(rows):
        u_scr[rows, :] = _norm_mod(h_ref[rows, :], g_ref[...], sh_ref[0], sc_ref[0]).astype(BF16)

    def matmuls(rows, first):
        u = u_scr[rows, :]
        acts = []
        for c0 in range(0, tf, MXU_N):
            a = jnp.dot(u, w1_ref[:, c0:c0 + MXU_N], preferred_element_type=F32)
            b = jnp.dot(u, w3_ref[:, c0:c0 + MXU_N], preferred_element_type=F32)
            acts.append((a * _sigmoid(a) * b).astype(BF16))
        down = jnp.dot(jnp.concatenate(acts, axis=1), w2_ref[...], preferred_element_type=F32)
        o_ref[rows, :] = down if first else o_ref[rows, :] + down

    def finish(rows):
        out = h_ref[rows, :] + (FFN_RES * gt_ref[0]) * o_ref[rows, :]
        if final:
            ms = jnp.mean(out * out, axis=-1, keepdims=True)
            out = out * lax.rsqrt(ms + EPS) * gf_ref[...]
        o_ref[rows, :] = out

    @pl.when(j == 0)
    def _():
        cast_next_weights()
        for rows in row_blocks:
            normalize(rows)
            matmuls(rows, True)

    @pl.when(jnp.logical_and(j > 0, j < nj - 1))
    def _():
        cast_next_weights()
        for rows in row_blocks:
            matmuls(rows, False)

    @pl.when(j == nj - 1)
    def _():
        cast_next_weights()
        for rows in row_blocks:
            matmuls(rows, False)
            finish(rows)


def _ffn(h, seq_len, g, shift, scale, gate, weights, next_weights=None, g_final=None, *,
         tm=1024, tf=512):
    w1, w3, w2 = weights
    m, d = h.shape
    dff = w1.shape[1]
    tm = min(tm, seq_len)
    ni, nj = m // tm, dff // tf
    assert nj >= 2 and tm % FFN_ROWS == 0
    per_batch = seq_len // tm
    final = g_final is not None
    cast_next = next_weights is not None
    vec = lambda i, j: (i // per_batch, 0, 0)
    in_specs = [
        pl.BlockSpec((tm, d), lambda i, j: (i, 0)),
        pl.BlockSpec((1, d), lambda i, j: (0, 0)),
        pl.BlockSpec((1, 1, d), vec),
        pl.BlockSpec((1, 1, d), vec),
        pl.BlockSpec((1, 1, d), vec),
        pl.BlockSpec((d, tf), lambda i, j: (0, j)),
        pl.BlockSpec((d, tf), lambda i, j: (0, j)),
        pl.BlockSpec((tf, d), lambda i, j: (j, 0)),
    ]
    args = [h, g.reshape(1, d), shift, scale, gate, w1, w3, w2]
    out_specs = [pl.BlockSpec((tm, d), lambda i, j: (i, 0))]
    out_shape = [jax.ShapeDtypeStruct((m, d), F32)]
    if cast_next:
        n1, n3, n2, layer = next_weights
        dr = d // ni
        assert d % ni == 0 and dr % LANES == 0
        in_specs += [pl.BlockSpec((None, dr, tf), lambda i, j: (layer, i, j)),
                     pl.BlockSpec((None, dr, tf), lambda i, j: (layer, i, j)),
                     pl.BlockSpec((None, tf, dr), lambda i, j: (layer, j, i))]
        args += [n1, n3, n2]
        out_specs += [pl.BlockSpec((dr, tf), lambda i, j: (i, j)),
                      pl.BlockSpec((dr, tf), lambda i, j: (i, j)),
                      pl.BlockSpec((tf, dr), lambda i, j: (j, i))]
        out_shape += [jax.ShapeDtypeStruct((d, dff), BF16), jax.ShapeDtypeStruct((d, dff), BF16),
                      jax.ShapeDtypeStruct((dff, d), BF16)]
    if final:
        in_specs.append(pl.BlockSpec((1, d), lambda i, j: (0, 0)))
        args.append(g_final.reshape(1, d))
    out = pl.pallas_call(
        functools.partial(_ffn_kernel, tf=tf, nj=nj, final=final, cast_next=cast_next),
        grid=(ni, nj),
        in_specs=in_specs,
        out_specs=out_specs,
        out_shape=out_shape,
        scratch_shapes=[pltpu.VMEM((tm, d), BF16)],
        compiler_params=_params("parallel", "arbitrary"),
        name="ffn",
    )(*args)
    return out[0], (tuple(out[1:]) if cast_next else None)


def _inproj_kernel(h_ref, g_ref, sh_ref, sc_ref, w_ref, wst_ref, o_ref, ost_ref, *, steps, s5_cols):
    u = _norm_mod(h_ref[...], g_ref[...], sh_ref[0], sc_ref[0]).astype(BF16)
    lo, hi = s5_cols
    o_ref[:, :lo] = jnp.dot(u, w_ref[:, :lo], preferred_element_type=F32)
    o_ref[:, lo:] = jnp.dot(u, w_ref[:, hi:], preferred_element_type=F32)
    su_t = lax.dot_general(wst_ref[...], u, NT_DIMS, preferred_element_type=F32)
    sub = su_t.shape[1] // LANES
    base = (pl.program_id(0) % steps) * sub
    for k in range(sub):
        ost_ref[:, pl.ds(base + k, 1), :] = su_t[:, k * LANES:(k + 1) * LANES][:, None, :]


def _inproj(h, seq_len, g, shift, scale, w, w_s5_t, layer, s5_cols, *, tm=256):
    m, d = h.shape
    w_s5 = w_s5_t.shape[1]
    d_pad = w.shape[2]
    d_out = d_pad - (s5_cols[1] - s5_cols[0])
    tm = min(tm, seq_len)
    per_batch = seq_len // tm
    steps = SUBLANES * LANES // tm
    vec = lambda i: (i // per_batch, 0, 0)
    return pl.pallas_call(
        functools.partial(_inproj_kernel, steps=steps, s5_cols=s5_cols),
        grid=(m // tm,),
        in_specs=[
            pl.BlockSpec((tm, d), lambda i: (i, 0)),
            pl.BlockSpec((1, d), lambda i: (0, 0)),
            pl.BlockSpec((1, 1, d), vec),
            pl.BlockSpec((1, 1, d), vec),
            pl.BlockSpec((None, d, d_pad), lambda i: (layer, 0, 0), pipeline_mode=pl.Buffered(1)),
            pl.BlockSpec((None, w_s5, d), lambda i: (layer, 0, 0), pipeline_mode=pl.Buffered(1)),
        ],
        out_specs=[pl.BlockSpec((tm, d_out), lambda i: (i, 0)),
                   pl.BlockSpec((w_s5, SUBLANES, LANES), lambda i: (0, i // steps, 0))],
        out_shape=[jax.ShapeDtypeStruct((m, d_out), F32),
                   jax.ShapeDtypeStruct((w_s5, m // LANES, LANES), F32)],
        compiler_params=_params("arbitrary"),
        name="inproj",
    )(h, g.reshape(1, d), shift, scale, w, w_s5_t)


def _ret_tables(seq_len):
    c = RET_CHUNK
    pos = jnp.arange(seq_len, dtype=F32)
    inv_freq = ROPE_BASE ** (-jnp.arange(0, RET_DK, 2, dtype=F32) / RET_DK)
    ang = pos[:, None] * inv_freq[None, :]
    cos, sin = jnp.cos(ang), jnp.sin(ang)
    cs = jnp.concatenate([cos, cos], axis=-1)
    sn = jnp.concatenate([-sin, sin], axis=-1)
    lg = jnp.log1p(-jnp.exp2(-5.0 - jnp.arange(RET_HEADS, dtype=F32)))
    idx = jnp.arange(c, dtype=F32)
    dist = jnp.abs(idx[:, None] - idx[None, :])
    dmat = jnp.exp(dist[None] * lg[:, None, None])
    rows = jnp.stack([c - 1.0 - idx, idx, idx + 1.0, c - idx])
    dvec = jnp.exp(rows[None] * lg[:, None, None])
    dvec = jnp.broadcast_to(dvec[..., None], dvec.shape + (RET_DK,))
    gc = jnp.broadcast_to(jnp.exp(c * lg)[:, None, None], (RET_HEADS, 1, RET_DV))
    return cs, sn, dmat, dvec, gc


def _ret_kernel(q_ref, k_ref, v_ref, g_ref, cs_ref, sn_ref, dmat_ref, dvec_ref, gc_ref, o_ref,
                q_scr, k_scr, qf_scr, qb_scr, kf_scr, kb_scr, of_scr, ob_scr, sf_scr, sb_scr,
                *, nchunks):
    c = RET_CHUNK
    scale = RET_DK ** -0.5
    chunk_rows = lambda n: pl.ds(pl.multiple_of(n * c, c), c)

    def prepare(n, carry):
        rows = chunk_rows(n)
        cs = cs_ref[rows, :]
        sn = sn_ref[rows, :]
        q = q_ref[0, rows, :]
        k = k_ref[0, rows, :]
        q = (q * cs + pltpu.roll(q, RET_DK // 2, 1) * sn) * scale
        k = k * cs + pltpu.roll(k, RET_DK // 2, 1) * sn
        q_scr[rows, :] = q.astype(BF16)
        k_scr[rows, :] = k.astype(BF16)
        kf_scr[rows, :] = (k * dvec_ref[0, 0]).astype(BF16)
        kb_scr[rows, :] = (k * dvec_ref[0, 1]).astype(BF16)
        qf_scr[rows, :] = (q * dvec_ref[0, 2]).astype(BF16)
        qb_scr[rows, :] = (q * dvec_ref[0, 3]).astype(BF16)
        return carry

    lax.fori_loop(0, nchunks, prepare, 0, unroll=4)
    sf_scr[...] = jnp.zeros_like(sf_scr)
    sb_scr[...] = jnp.zeros_like(sb_scr)
    gc = gc_ref[0]

    def recur(n, carry):
        rf = chunk_rows(n)
        vf = v_ref[0, rf, :].astype(BF16)
        scores = lax.dot_general(q_scr[rf, :], k_scr[rf, :], NT_DIMS,
                                 preferred_element_type=F32) * dmat_ref[0]
        s = sf_scr[...]
        of_scr[rf, :] = (jnp.dot(scores.astype(BF16), vf, preferred_element_type=F32)
                         + jnp.dot(qf_scr[rf, :], s.astype(BF16), preferred_element_type=F32))
        sf_scr[...] = gc * s + lax.dot_general(kf_scr[rf, :], vf, TN_DIMS,
                                               preferred_element_type=F32)
        rb = chunk_rows(nchunks - 1 - n)
        vb = v_ref[0, rb, :].astype(BF16)
        s = sb_scr[...]
        ob_scr[rb, :] = jnp.dot(qb_scr[rb, :], s.astype(BF16), preferred_element_type=F32)
        sb_scr[...] = gc * s + lax.dot_general(kb_scr[rb, :], vb, TN_DIMS,
                                               preferred_element_type=F32)
        return carry

    lax.fori_loop(0, nchunks, recur, 0, unroll=4)

    def finish(n, carry):
        rows = chunk_rows(n)
        o = of_scr[rows, :] + ob_scr[rows, :]
        oc = o - jnp.mean(o, axis=-1, keepdims=True)
        ln = oc * lax.rsqrt(jnp.mean(oc * oc, axis=-1, keepdims=True) + EPS)
        g = g_ref[0, rows, :]
        o_ref[0, rows, :] = (g * _sigmoid(g) * ln).astype(o_ref.dtype)
        return carry

    lax.fori_loop(0, nchunks, finish, 0, unroll=4)


def _retention(proj, tables):
    bsz, seq_len, _ = proj.shape
    cs, sn, dmat, dvec, gc = tables
    h = RET_HEADS
    kq, kk = 0, h
    kv, kg = (2 * h * RET_DK) // RET_DV, (2 * h * RET_DK + h * RET_DV) // RET_DV
    tok = lambda off: (lambda b, i: (b, 0, off + i))
    return pl.pallas_call(
        functools.partial(_ret_kernel, nchunks=seq_len // RET_CHUNK),
        grid=(bsz, h),
        in_specs=[
            pl.BlockSpec((1, seq_len, RET_DK), tok(kq)),
            pl.BlockSpec((1, seq_len, RET_DK), tok(kk)),
            pl.BlockSpec((1, seq_len, RET_DV), tok(kv)),
            pl.BlockSpec((1, seq_len, RET_DV), tok(kg)),
            pl.BlockSpec((seq_len, RET_DK), lambda b, i: (0, 0), pipeline_mode=pl.Buffered(1)),
            pl.BlockSpec((seq_len, RET_DK), lambda b, i: (0, 0), pipeline_mode=pl.Buffered(1)),
            pl.BlockSpec((1, RET_CHUNK, RET_CHUNK), lambda b, i: (i, 0, 0)),
            pl.BlockSpec((1, 4, RET_CHUNK, RET_DK), lambda b, i: (i, 0, 0, 0)),
            pl.BlockSpec((1, 1, RET_DV), lambda b, i: (i, 0, 0)),
        ],
        out_specs=pl.BlockSpec((1, seq_len, RET_DV), lambda b, i: (b, 0, i)),
        out_shape=jax.ShapeDtypeStruct((bsz, seq_len, h * RET_DV), BF16),
        scratch_shapes=(
            [pltpu.VMEM((seq_len, RET_DK), BF16)] * 6
            + [pltpu.VMEM((seq_len, RET_DV), F32)] * 2
            + [pltpu.VMEM((RET_DK, RET_DV), F32)] * 2
        ),
        compiler_params=_params("parallel", "parallel"),
        name="retention",
    )(proj, proj, proj, proj, cs, sn, dmat, dvec, gc)


def _gla_kernel(q_ref, k_ref, v_ref, g_ref, lr_ref, wf_ref, wb_ref, bf_ref, bb_ref, o_ref,
                q0_scr, q1_scr, kin_scr, kst_scr, dec_scr, o_scr, st_scr, *, nblocks):
    c, r = GLA_CHUNK, GLA_BLOCK
    per = r // c
    dk2, dv2 = 2 * GLA_DK, 2 * GLA_DV
    scale = GLA_DK ** -0.5
    block_rows = lambda m: pl.ds(pl.multiple_of(m * r, r), r)

    ri = lax.broadcasted_iota(jnp.int32, (r, r), 0)
    ci = lax.broadcasted_iota(jnp.int32, (r, r), 1)
    same_chunk = (ri // c) == (ci // c)
    lane = lax.broadcasted_iota(jnp.int32, (r, dk2), 1)
    v_lane = lax.broadcasted_iota(jnp.int32, (r, dv2), 1)
    se = lax.broadcasted_iota(jnp.int32, (dv2, dk2), 0)
    sd = lax.broadcasted_iota(jnp.int32, (dv2, dk2), 1)
    same_head = jnp.where(se < GLA_DV, jnp.where(sd < GLA_DK, 1.0, 0.0),
                          jnp.where(sd >= GLA_DK, 1.0, 0.0)).astype(F32)

    def direction(w_ref, b_ref, reverse):
        if reverse:
            cum = same_chunk & (ci >= ri)
            keep = same_chunk & (ci > ri)
        else:
            cum = same_chunk & (ci <= ri)
            keep = same_chunk & (ci <= ri)
        sum_mat = jnp.concatenate([jnp.where(cum, 1.0, 0.0), jnp.where(same_chunk, 1.0, 0.0)],
                                  axis=0).astype(BF16)

        def prepare(m, carry):
            rows = block_rows(m)
            lr = lr_ref[0, rows, :]
            lr_hi = lr.astype(BF16)
            lr_mid = (lr - lr_hi.astype(F32)).astype(BF16)
            x = jnp.dot(jnp.concatenate([lr_hi, lr_mid, lr_hi], axis=1), w_ref[0],
                        preferred_element_type=F32) + b_ref[0]
            la = (jnp.minimum(x, 0.0) - jnp.log(1.0 + jnp.exp(-jnp.abs(x)))) * (1.0 / GLA_TAU)
            hi = la.astype(BF16)
            rest = la - hi.astype(F32)
            mid = rest.astype(BF16)
            lo = (rest - mid.astype(F32)).astype(BF16)
            sums = jnp.dot(sum_mat, jnp.concatenate([hi, mid, lo], axis=1),
                           preferred_element_type=F32)
            sums = sums[:, :dk2] + sums[:, dk2:2 * dk2] + sums[:, 2 * dk2:]
            b, total = sums[:r], sums[r:]
            q_in = q_ref[0, rows, :] * scale * jnp.exp(b)
            k = k_ref[0, rows, :]
            q0_scr[rows, :] = jnp.where(lane < GLA_DK, q_in, 0.0).astype(BF16)
            q1_scr[rows, :] = jnp.where(lane >= GLA_DK, q_in, 0.0).astype(BF16)
            kin_scr[rows, :] = (k * jnp.exp(-b)).astype(BF16)
            kst_scr[rows, :] = (k * jnp.exp(total - b)).astype(BF16)
            dec_scr[rows, :] = jnp.exp(total)
            return carry

        lax.fori_loop(0, nblocks, prepare, 0, unroll=4)
        st_scr[...] = jnp.zeros_like(st_scr)

        def recur(t, carry):
            rows = block_rows(nblocks - 1 - t if reverse else t)
            q0 = q0_scr[rows, :]
            q1 = q1_scr[rows, :]
            kin = kin_scr[rows, :]
            kst = kst_scr[rows, :]
            dec = dec_scr[rows, :]
            vb = v_ref[0, rows, :].astype(BF16)
            scores = [jnp.where(keep, lax.dot_general(qh, kin, NT_DIMS, preferred_element_type=F32),
                                0.0).astype(BF16) for qh in (q0, q1)]
            v_heads = jnp.concatenate([jnp.where(v_lane < GLA_DV, vb, jnp.zeros_like(vb)),
                                       jnp.where(v_lane >= GLA_DV, vb, jnp.zeros_like(vb))], axis=0)
            intra = jnp.dot(jnp.concatenate(scores, axis=1), v_heads, preferred_element_type=F32)
            qin = q0 + q1
            st = st_scr[...]
            inter = [None] * per
            for i in (range(per - 1, -1, -1) if reverse else range(per)):
                sl = slice(i * c, (i + 1) * c)
                inter[i] = lax.dot_general(qin[sl], st.astype(BF16), NT_DIMS,
                                           preferred_element_type=F32)
                kv_t = lax.dot_general(vb[sl], kst[sl], TN_DIMS, preferred_element_type=F32)
                st = st * dec[i * c:i * c + 1, :] + kv_t * same_head
            st_scr[...] = st
            o = intra + jnp.concatenate(inter, axis=0)
            if not reverse:
                o_scr[rows, :] = o
            else:
                o = o + o_scr[rows, :]
                normed = []
                for hh in range(2):
                    oh = o[:, hh * GLA_DV:(hh + 1) * GLA_DV]
                    normed.append(oh * lax.rsqrt(jnp.mean(oh * oh, axis=-1, keepdims=True) + EPS))
                g = g_ref[0, rows, :]
                o_ref[0, rows, :] = (g * _sigmoid(g)
                                     * jnp.concatenate(normed, axis=1)).astype(o_ref.dtype)
            return carry

        lax.fori_loop(0, nblocks, recur, 0, unroll=4)

    direction(wf_ref, bf_ref, False)
    direction(wb_ref, bb_ref, True)


def _gla(proj, w_gate, b_gate, col0):
    bsz, seq_len, _ = proj.shape
    pairs = GLA_HEADS // 2
    dk2, dv2 = 2 * GLA_DK, 2 * GLA_DV
    c_q = col0
    c_k = c_q + GLA_HEADS * GLA_DK
    c_v = c_k + GLA_HEADS * GLA_DK
    c_g = c_v + GLA_HEADS * GLA_DV
    c_lr = c_g + GLA_HEADS * GLA_DV
    wg = w_gate.astype(F32).reshape(2, GLA_RANK, pairs, dk2).transpose(0, 2, 1, 3)
    wf = jnp.zeros((pairs, LANES, dk2), F32).at[:, :GLA_RANK].set(wg[0])
    wb = jnp.zeros((pairs, LANES, dk2), F32).at[:, GLA_RANK:2 * GLA_RANK].set(wg[1])

    def split3(w):
        w_hi = w.astype(BF16)
        w_mid = (w - w_hi.astype(F32)).astype(BF16)
        return jnp.concatenate([w_hi, w_hi, w_mid], axis=1)

    wf, wb = split3(wf), split3(wb)
    bg = b_gate.astype(F32).reshape(2, pairs, 1, dk2)
    tok = lambda col, width: (lambda b, p: (b, 0, col // width + p))
    mat = lambda b, p: (p, 0, 0)
    return pl.pallas_call(
        functools.partial(_gla_kernel, nblocks=seq_len // GLA_BLOCK),
        grid=(bsz, pairs),
        in_specs=[
            pl.BlockSpec((1, seq_len, dk2), tok(c_q, dk2)),
            pl.BlockSpec((1, seq_len, dk2), tok(c_k, dk2)),
            pl.BlockSpec((1, seq_len, dv2), tok(c_v, dv2)),
            pl.BlockSpec((1, seq_len, dv2), tok(c_g, dv2)),
            pl.BlockSpec((1, seq_len, LANES), lambda b, p: (b, 0, c_lr // LANES)),
            pl.BlockSpec((1, 3 * LANES, dk2), mat),
            pl.BlockSpec((1, 3 * LANES, dk2), mat),
            pl.BlockSpec((1, 1, dk2), mat),
            pl.BlockSpec((1, 1, dk2), mat),
        ],
        out_specs=pl.BlockSpec((1, seq_len, dv2), lambda b, p: (b, 0, p)),
        out_shape=jax.ShapeDtypeStruct((bsz, seq_len, GLA_HEADS * GLA_DV), BF16),
        scratch_shapes=(
            [pltpu.VMEM((seq_len, dk2), BF16)] * 4
            + [pltpu.VMEM((seq_len, dk2), F32), pltpu.VMEM((seq_len, dv2), F32),
               pltpu.VMEM((dv2, dk2), F32)]
        ),
        compiler_params=_params("parallel", "parallel"),
        name="gla",
    )(proj, proj, proj, proj, proj, wf, wb, bg[0], bg[1])


def _swap(x):
    return pltpu.roll(x, S5_STATE, 1)


def _s5_prep_kernel(lre_ref, lim_ref, ldt_ref, brr_ref, bis_ref, cr_ref, ci_ref,
                    m_ref, e_ref, ft_ref, a_ref):
    t_len = S5_CHUNK
    gsz = S5_GROUP
    width = gsz * t_len
    first = lax.broadcasted_iota(jnp.int32, (1, LANES), 1) < S5_STATE
    one_zero = jnp.where(first, 1.0, 0.0).astype(F32)
    sign = jnp.where(first, 1.0, -1.0).astype(F32)

    def packed_exp(t, are, aim):
        ph = t * aim
        return jnp.exp(t * are) * jnp.where(first, jnp.cos(ph), jnp.sin(ph))

    def dup_re(x):
        return jnp.where(first, x, _swap(x))

    def dup_im_signed(x):
        return jnp.where(first, -_swap(x), x)

    def per_channel(pw, w_a, w_b):
        return jnp.concatenate([pw * w_a[c:c + 1, :] + _swap(pw) * w_b[c:c + 1, :]
                                for c in range(gsz)], axis=0)

    trow = lax.broadcasted_iota(jnp.int32, (t_len, 1), 0).astype(F32)
    panels = []
    for d in range(2):
        lre = lre_ref[0, d:d + 1, :]
        lim = lim_ref[0, d:d + 1, :]
        dt = jnp.exp(ldt_ref[0, d:d + 1, :])
        are, aim = lre * dt, lim * dt
        num = packed_exp(jnp.ones((1, 1), F32), are, aim) - one_zero
        inv_den = 1.0 / (lre * lre + lim * lim)
        coef = num * (lre * inv_den) + _swap(num) * (lim * inv_den * sign)
        bbar = coef * brr_ref[0, d] + _swap(coef) * bis_ref[0, d]
        if d == 0:
            t_e, t_f, t_g = t_len - 1.0 - trow, trow + 1.0, trow
        else:
            t_e, t_f, t_g = trow, t_len - trow, t_len - 1.0 - trow
        e_mat = per_channel(packed_exp(t_e, are, aim), dup_re(bbar), dup_im_signed(bbar))
        e_ref[0, :, d * LANES:(d + 1) * LANES] = e_mat.astype(BF16)
        cr, ci = cr_ref[0, d], ci_ref[0, d]
        ft_ref[0, d] = per_channel(packed_exp(t_f, are, aim), cr, ci).astype(BF16)
        g_mat = per_channel(packed_exp(t_g, are, aim), cr, ci)
        panels.append(lax.dot_general(bbar, g_mat, NT_DIMS, preferred_element_type=F32,
                                      precision=HIGHEST))
        kcol = lax.broadcasted_iota(jnp.int32, (8, 1), 0)
        tk = (t_len * jnp.left_shift(1, kcol)).astype(F32)
        ak = packed_exp(tk, are, aim)
        a_ref[0, d, 0:8, :] = dup_re(ak)
        a_ref[0, d, 8:16, :] = dup_im_signed(ak)

    p_fwd, p_bwd = panels
    j = lax.broadcasted_iota(jnp.int32, (t_len, width), 0)
    pos = lax.broadcasted_iota(jnp.int32, (t_len, width), 1) % t_len
    for c in range(gsz):
        fwd = pltpu.roll(jnp.broadcast_to(p_fwd[c:c + 1, :], (t_len, width)), 0, 1,
                         stride=1, stride_axis=0)
        bwd = pltpu.roll(jnp.broadcast_to(p_bwd[c:c + 1, :], (t_len, width)), width - (t_len - 1), 1,
                         stride=1, stride_axis=0)
        blk = jnp.where(pos >= j, fwd, 0.0) + jnp.where(pos <= j, bwd, 0.0)
        m_ref[0, c * t_len:(c + 1) * t_len, :] = blk.astype(BF16)


def _s5_apply_kernel(u_ref, m_ref, e_ref, ft_ref, a_ref, d_ref, y_ref, *, blocks_per_seq, nsteps):
    t_len = S5_CHUNK
    per = LANES // t_len
    nblk = u_ref.shape[1]
    xc = [u_ref[c] for c in range(S5_GROUP)]
    ys, sums = [], []
    for q in range(per):
        u = jnp.concatenate([x[:, q * t_len:(q + 1) * t_len] for x in xc], axis=1)
        ub = u.astype(BF16)
        ys.append(jnp.dot(ub, m_ref[0], preferred_element_type=F32) + u * d_ref[0])
        sums.append(jnp.dot(ub, e_ref[0], preferred_element_type=F32))
    nidx = lax.broadcasted_iota(jnp.int32, (nblk, LANES), 0) % blocks_per_seq
    for d in range(2):

        def block_shift(val, sh):
            if d == 0:
                return jnp.where(nidx >= sh, pltpu.roll(val, sh, 0), 0.0)
            return jnp.where(nidx < blocks_per_seq - sh, pltpu.roll(val, nblk - sh, 0), 0.0)

        def chunk_shift(x, sh):
            if sh % per == 0:
                return [block_shift(v, sh // per) for v in x]
            src = [q - sh if d == 0 else q + sh for q in range(per)]
            return [x[s] if 0 <= s < per else block_shift(x[s % per], 1) for s in src]

        x = [s[:, d * LANES:(d + 1) * LANES] for s in sums]
        for k in range(nsteps):
            xs = chunk_shift(x, 1 << k)
            x = [v + a_ref[0, d, k:k + 1, :] * w + a_ref[0, d, 8 + k:9 + k, :] * _swap(w)
                 for v, w in zip(x, xs)]
        for q, v in enumerate(chunk_shift(x, 1)):
            ys[q] = ys[q] + lax.dot_general(v.astype(BF16), ft_ref[0, d], NT_DIMS,
                                            preferred_element_type=F32)
    for c in range(S5_GROUP):
        y_ref[c] = jnp.concatenate([y[:, c * t_len:(c + 1) * t_len] for y in ys], axis=1)


def _s5(su_t, seq_len, lam_re, lam_im, log_dt, b_re, b_im, c_re, c_im, d_skip):
    w, nblk, _ = su_t.shape
    g = w // S5_GROUP
    t_len = S5_CHUNK
    nchunks = seq_len // t_len
    nsteps = max(1, (nchunks - 1).bit_length())
    assert nsteps <= 8 and seq_len % LANES == 0
    width = S5_GROUP * t_len
    dup = lambda a: jnp.concatenate([a, a], axis=-1)
    gd = lambda a: jnp.moveaxis(a.astype(F32), 0, 1)
    lre = dup(gd(lam_re))
    lim = dup(gd(lam_im))
    ldt = jnp.broadcast_to(gd(log_dt)[..., None], (g, 2, LANES))
    bt_re = jnp.swapaxes(gd(b_re), -1, -2)
    bt_im = jnp.swapaxes(gd(b_im), -1, -2)
    brr = jnp.concatenate([bt_re, bt_re], axis=-1)
    bis = jnp.concatenate([-bt_im, bt_im], axis=-1)
    cre, cim = gd(c_re), gd(c_im)
    cr = jnp.concatenate([cre, -cre], axis=-1)
    ci = jnp.concatenate([-cim, -cim], axis=-1)
    vec3 = pl.BlockSpec((1, 2, LANES), lambda i: (i, 0, 0))
    mat4 = pl.BlockSpec((1, 2, S5_GROUP, LANES), lambda i: (i, 0, 0, 0))
    m_mat, e_mat, ft_mat, a_mat = pl.pallas_call(
        _s5_prep_kernel,
        grid=(g,),
        in_specs=[vec3, vec3, vec3, mat4, mat4, mat4, mat4],
        out_specs=[
            pl.BlockSpec((1, width, width), lambda i: (i, 0, 0)),
            pl.BlockSpec((1, width, 2 * LANES), lambda i: (i, 0, 0)),
            pl.BlockSpec((1, 2, width, LANES), lambda i: (i, 0, 0, 0)),
            pl.BlockSpec((1, 2, 16, LANES), lambda i: (i, 0, 0, 0)),
        ],
        out_shape=[
            jax.ShapeDtypeStruct((g, width, width), BF16),
            jax.ShapeDtypeStruct((g, width, 2 * LANES), BF16),
            jax.ShapeDtypeStruct((g, 2, width, LANES), BF16),
            jax.ShapeDtypeStruct((g, 2, 16, LANES), F32),
        ],
        compiler_params=_params("parallel"),
        name="s5_prep",
    )(lre, lim, ldt, brr, bis, cr, ci)
    d_rep = jnp.repeat(d_skip.astype(F32).reshape(g, 1, S5_GROUP), t_len, axis=-1)
    blk = pl.BlockSpec((S5_GROUP, nblk, LANES), lambda i: (i, 0, 0))
    return pl.pallas_call(
        functools.partial(_s5_apply_kernel, blocks_per_seq=seq_len // LANES, nsteps=nsteps),
        grid=(g,),
        in_specs=[
            blk,
            pl.BlockSpec((1, width, width), lambda i: (i, 0, 0)),
            pl.BlockSpec((1, width, 2 * LANES), lambda i: (i, 0, 0)),
            pl.BlockSpec((1, 2, width, LANES), lambda i: (i, 0, 0, 0)),
            pl.BlockSpec((1, 2, 16, LANES), lambda i: (i, 0, 0, 0)),
            pl.BlockSpec((1, 1, width), lambda i: (i, 0, 0)),
        ],
        out_specs=blk,
        out_shape=jax.ShapeDtypeStruct(su_t.shape, F32),
        compiler_params=_params("parallel"),
        name="s5_apply",
    )(su_t, m_mat, e_mat, ft_mat, a_mat, d_rep)


def _outproj_kernel(h_ref, gt_ref, yr_ref, ys_ref, yg_ref, wglu_ref, bglu_ref, wout_ref, o_ref,
                    *, steps):
    sub = o_ref.shape[0] // LANES
    base = (pl.program_id(0) % steps) * sub
    y = jnp.concatenate([ys_ref[:, pl.ds(base + k, 1), :][:, 0, :] for k in range(sub)], axis=1)
    z = y * (0.5 * (1.0 + jnp.tanh(math.sqrt(2.0 / math.pi) * (y + 0.044715 * (y * y * y)))))
    gl = jnp.dot(wglu_ref[...], z.astype(BF16), preferred_element_type=F32) + bglu_ref[...]
    s5 = (z * _sigmoid(gl)).astype(BF16)
    r_end = yr_ref.shape[1]
    s_end = r_end + s5.shape[0]
    acc = jnp.dot(yr_ref[...], wout_ref[:r_end, :], preferred_element_type=F32)
    acc = acc + jnp.dot(yg_ref[...], wout_ref[s_end:, :], preferred_element_type=F32)
    acc = acc + lax.dot_general(s5, wout_ref[r_end:s_end, :], TN_DIMS, preferred_element_type=F32)
    o_ref[...] = h_ref[...] + gt_ref[0] * acc


def _outproj(h, seq_len, gate, y_ret, y_s5_t, y_gla, wglu_t, b_glu, w_out, layer, *, tm=512):
    m, d = h.shape
    tm = min(tm, seq_len)
    per_batch = seq_len // tm
    steps = SUBLANES * LANES // tm
    w_s5 = y_s5_t.shape[0]
    row = lambda width: pl.BlockSpec((tm, width), lambda i: (i, 0))
    whole = lambda a: pl.BlockSpec((None,) + a.shape[1:], lambda i: (layer, 0, 0),
                                   pipeline_mode=pl.Buffered(1))
    bglu = b_glu.astype(F32).reshape(-1, w_s5, 1)
    return pl.pallas_call(
        functools.partial(_outproj_kernel, steps=steps),
        grid=(m // tm,),
        in_specs=[
            row(d),
            pl.BlockSpec((1, 1, d), lambda i: (i // per_batch, 0, 0)),
            row(y_ret.shape[1]),
            pl.BlockSpec((w_s5, SUBLANES, LANES), lambda i: (0, i // steps, 0)),
            row(y_gla.shape[1]),
            whole(wglu_t), whole(bglu), whole(w_out),
        ],
        out_specs=row(d),
        out_shape=jax.ShapeDtypeStruct((m, d), F32),
        compiler_params=_params("parallel"),
        name="outproj",
    )(h, gate, y_ret, y_s5_t, y_gla, wglu_t, bglu, w_out)


def kernel(x, c, w_ada, b_ada, g_ffn1, ffn1_w1, ffn1_w3, ffn1_w2, g_mix, w_in, s5_lam_re, s5_lam_im, s5_log_dt, s5_b_re, s5_b_im, s5_c_re, s5_c_im, s5_d, s5_w_glu, s5_b_glu, gla_w_gate, gla_b_gate, w_out, g_ffn2, ffn2_w1, ffn2_w3, ffn2_w2, g_final):
    bsz, seq_len, d = x.shape
    depth = w_ada.shape[0]
    mod = _ada_mod(c, w_ada, b_ada).reshape(depth, bsz, N_MOD, 1, d)
    tables = _ret_tables(seq_len)
    ret_w = RET_HEADS * (2 * RET_DK + 2 * RET_DV)
    s5_w = s5_d.shape[1]
    h = x.reshape(bsz * seq_len, d)
    ffn_f32 = {0: (ffn1_w1, ffn1_w3, ffn1_w2), 1: (ffn2_w1, ffn2_w3, ffn2_w2)}
    ffn_w = tuple(w[0].astype(BF16) for w in ffn_f32[0])
    s5_cols = (ret_w, ret_w + s5_w)
    w_s5_t = jnp.swapaxes(w_in[:, :, s5_cols[0]:s5_cols[1]], 1, 2).astype(BF16)
    w_in = jnp.pad(w_in.astype(BF16), ((0, 0), (0, 0), (0, -w_in.shape[2] % LANES)))
    w_out = w_out.astype(BF16)
    w_glu_t = jnp.swapaxes(s5_w_glu.astype(BF16), 1, 2)
    for l in range(depth):
        sh1, sc1, gt1, sh2, sc2, gt2, sh3, sc3, gt3 = (mod[l, :, i] for i in range(N_MOD))
        h, ffn_w = _ffn(h, seq_len, g_ffn1[l], sh1, sc1, gt1, ffn_w, ffn_f32[1] + (l,))
        proj, su_t = _inproj(h, seq_len, g_mix[l], sh2, sc2, w_in, w_s5_t, l, s5_cols)
        proj = proj.reshape(bsz, seq_len, -1)
        y_ret = _retention(proj, tables)
        y_s5_t = _s5(su_t, seq_len, s5_lam_re[l], s5_lam_im[l], s5_log_dt[l],
                     s5_b_re[l], s5_b_im[l], s5_c_re[l], s5_c_im[l], s5_d[l])
        y_gla = _gla(proj, gla_w_gate[l], gla_b_gate[l], ret_w)
        h = _outproj(h, seq_len, gt2, y_ret.reshape(bsz * seq_len, -1), y_s5_t,
                     y_gla.reshape(bsz * seq_len, -1), w_glu_t, s5_b_glu, w_out, l)
        last = l == depth - 1
        h, ffn_w = _ffn(h, seq_len, g_ffn2[l], sh3, sc3, gt3, ffn_w,
                        None if last else ffn_f32[0] + (l + 1,), g_final if last else None)
    return h.reshape(bsz, seq_len, d)
```

```python
import functools
import math

import jax
import jax.numpy as jnp
from jax import lax
from jax.experimental import pallas as pl
from jax.experimental.pallas import tpu as pltpu

F32 = jnp.float32
BF16 = jnp.bfloat16
HIGHEST = lax.Precision.HIGHEST

RET_HEADS = 4
RET_DK = 128
RET_DV = 256
RET_CHUNK = 256
ROPE_BASE = 10000.0
S5_GROUP = 16
S5_STATE = 64
S5_CHUNK = 32
GLA_HEADS = 4
GLA_DK = 64
GLA_DV = 128
GLA_RANK = 16
GLA_TAU = 16.0
GLA_CHUNK = 64
GLA_BLOCK = 256
FFN_RES = 0.5
FFN_ROWS = 512
N_MOD = 9
EPS = 1e-6

LANES = 128
SUBLANES = 8
MXU_N = 256
VMEM_LIMIT = 60 << 20

NT_DIMS = (((1,), (1,)), ((), ()))
TN_DIMS = (((0,), (0,)), ((), ()))


def _params(*semantics):
    return pltpu.CompilerParams(dimension_semantics=semantics, vmem_limit_bytes=VMEM_LIMIT)


def _sigmoid(x):
    return 1.0 / (1.0 + jnp.exp(-x))


def _norm_mod(x, g, shift, scale):
    ms = jnp.mean(x * x, axis=-1, keepdims=True)
    return x * lax.rsqrt(ms + EPS) * (g * (1.0 + scale)) + shift


def _ada_kernel(c_ref, w_ref, b_ref, o_ref):
    c = c_ref[...]
    cond = c * _sigmoid(c)
    rows = cond.shape[0]
    c_hi = cond.astype(BF16).astype(F32)
    lhs = jnp.concatenate([c_hi, cond - c_hi], axis=0).astype(BF16)
    w = w_ref[0]
    w_hi = w.astype(BF16)
    w_mid = (w - w_hi.astype(F32)).astype(BF16)
    s = (jnp.dot(lhs, w_hi, preferred_element_type=F32)
         + jnp.dot(lhs, w_mid, preferred_element_type=F32))
    o_ref[0] = s[:rows] + s[rows:] + b_ref[0]


def _ada_mod(c, w_ada, b_ada):
    depth, d, n = w_ada.shape
    bsz = c.shape[0]
    rows = -(-bsz // 8) * 8
    c_pad = jnp.zeros((rows, d), F32).at[:bsz].set(c)
    tn = n // 8
    out = pl.pallas_call(
        _ada_kernel,
        grid=(depth, n // tn),
        in_specs=[
            pl.BlockSpec((rows, d), lambda l, j: (0, 0)),
            pl.BlockSpec((1, d, tn), lambda l, j: (l, 0, j)),
            pl.BlockSpec((1, 1, tn), lambda l, j: (l, 0, j)),
        ],
        out_specs=pl.BlockSpec((1, rows, tn), lambda l, j: (l, 0, j)),
        out_shape=jax.ShapeDtypeStruct((depth, rows, n), F32),
        compiler_params=_params("parallel", "parallel"),
        name="ada_mod",
    )(c_pad, w_ada, b_ada.reshape(depth, 1, n))
    return out[:, :bsz]


def _ffn_kernel(h_ref, g_ref, sh_ref, sc_ref, gt_ref, w1_ref, w3_ref, w2_ref, *rest, tf, nj, final,
                cast_next):
    rest = list(rest)
    next_f32 = [rest.pop(0) for _ in range(3)] if cast_next else []
    gf_ref = rest.pop(0) if final else None
    o_ref = rest.pop(0)
    next_bf16 = [rest.pop(0) for _ in range(3)] if cast_next else []
    (u_scr,) = rest
    j = pl.program_id(1)

    def cast_next_weights():
        for src, dst in zip(next_f32, next_bf16):
            dst[...] = src[...].astype(BF16)

    row_blocks = [slice(r0, r0 + FFN_ROWS) for r0 in range(0, o_ref.shape[0], FFN_ROWS)]

    def normalize(rows):
        u_scr[rows, :] = _norm_mod(h_ref[rows, :], g_ref[...], sh_ref[0], sc_ref[0]).astype(BF16)

    def matmuls(rows, first):
        u = u_scr[rows, :]
        acts = []
        for c0 in range(0, tf, MXU_N):
            a = jnp.dot(u, w1_ref[:, c0:c0 + MXU_N], preferred_element_type=F32)
            b = jnp.dot(u, w3_ref[:, c0:c0 + MXU_N], preferred_element_type=F32)
            acts.append((a * _sigmoid(a) * b).astype(BF16))
        down = jnp.dot(jnp.concatenate(acts, axis=1), w2_ref[...], preferred_element_type=F32)
        o_ref[rows, :] = down if first else o_ref[rows, :] + down

    def finish(rows):
        out = h_ref[rows, :] + (FFN_RES * gt_ref[0]) * o_ref[rows, :]
        if final:
            ms = jnp.mean(out * out, axis=-1, keepdims=True)
            out = out * lax.rsqrt(ms + EPS) * gf_ref[...]
        o_ref[rows, :] = out

    @pl.when(j == 0)
    def _():
        cast_next_weights()
        for rows in row_blocks:
            normalize(rows)
            matmuls(rows, True)

    @pl.when(jnp.logical_and(j > 0, j < nj - 1))
    def _():
        cast_next_weights()
        for rows in row_blocks:
            matmuls(rows, False)

    @pl.when(j == nj - 1)
    def _():
        cast_next_weights()
        for rows in row_blocks:
            matmuls(rows, False)
            finish(rows)


def _ffn(h, seq_len, g, shift, scale, gate, weights, next_weights=None, g_final=None, *,
         tm=1024, tf=512):
    w1, w3, w2 = weights
    m, d = h.shape
    dff = w1.shape[1]
    tm = min(tm, seq_len)
    ni, nj = m // tm, dff // tf
    assert nj >= 2 and tm % FFN_ROWS == 0
    per_batch = seq_len // tm
    final = g_final is not None
    cast_next = next_weights is not None
    vec = lambda i, j: (i // per_batch, 0, 0)
    in_specs = [
        pl.BlockSpec((tm, d), lambda i, j: (i, 0)),
        pl.BlockSpec((1, d), lambda i, j: (0, 0)),
        pl.BlockSpec((1, 1, d), vec),
        pl.BlockSpec((1, 1, d), vec),
        pl.BlockSpec((1, 1, d), vec),
        pl.BlockSpec((d, tf), lambda i, j: (0, j)),
        pl.BlockSpec((d, tf), lambda i, j: (0, j)),
        pl.BlockSpec((tf, d), lambda i, j: (j, 0)),
    ]
    args = [h, g.reshape(1, d), shift, scale, gate, w1, w3, w2]
    out_specs = [pl.BlockSpec((tm, d), lambda i, j: (i, 0))]
    out_shape = [jax.ShapeDtypeStruct((m, d), F32)]
    if cast_next:
        n1, n3, n2, layer = next_weights
        dr = d // ni
        assert d % ni == 0 and dr % LANES == 0
        in_specs += [pl.BlockSpec((None, dr, tf), lambda i, j: (layer, i, j)),
                     pl.BlockSpec((None, dr, tf), lambda i, j: (layer, i, j)),
                     pl.BlockSpec((None, tf, dr), lambda i, j: (layer, j, i))]
        args += [n1, n3, n2]
        out_specs += [pl.BlockSpec((dr, tf), lambda i, j: (i, j)),
                      pl.BlockSpec((dr, tf), lambda i, j: (i, j)),
                      pl.BlockSpec((tf, dr), lambda i, j: (j, i))]
        out_shape += [jax.ShapeDtypeStruct((d, dff), BF16), jax.ShapeDtypeStruct((d, dff), BF16),
                      jax.ShapeDtypeStruct((dff, d), BF16)]
    if final:
        in_specs.append(pl.BlockSpec((1, d), lambda i, j: (0, 0)))
        args.append(g_final.reshape(1, d))
    out = pl.pallas_call(
        functools.partial(_ffn_kernel, tf=tf, nj=nj, final=final, cast_next=cast_next),
        grid=(ni, nj),
        in_specs=in_specs,
        out_specs=out_specs,
        out_shape=out_shape,
        scratch_shapes=[pltpu.VMEM((tm, d), BF16)],
        compiler_params=_params("parallel", "arbitrary"),
        name="ffn",
    )(*args)
    return out[0], (tuple(out[1:]) if cast_next else None)


def _inproj_kernel(h_ref, g_ref, sh_ref, sc_ref, w_ref, wst_ref, o_ref, ost_ref, *, steps, s5_cols):
    u = _norm_mod(h_ref[...], g_ref[...], sh_ref[0], sc_ref[0]).astype(BF16)
    lo, hi = s5_cols
    o_ref[:, :lo] = jnp.dot(u, w_ref[:, :lo], preferred_element_type=F32)
    o_ref[:, lo:] = jnp.dot(u, w_ref[:, hi:], preferred_element_type=F32)
    su_t = lax.dot_general(wst_ref[...], u, NT_DIMS, preferred_element_type=F32)
    sub = su_t.shape[1] // LANES
    base = (pl.program_id(0) % steps) * sub
    for k in range(sub):
        ost_ref[:, pl.ds(base + k, 1), :] = su_t[:, k * LANES:(k + 1) * LANES][:, None, :]


def _inproj(h, seq_len, g, shift, scale, w, w_s5_t, layer, s5_cols, *, tm=256):
    m, d = h.shape
    w_s5 = w_s5_t.shape[1]
    d_pad = w.shape[2]
    d_out = d_pad - (s5_cols[1] - s5_cols[0])
    tm = min(tm, seq_len)
    per_batch = seq_len // tm
    steps = SUBLANES * LANES // tm
    vec = lambda i: (i // per_batch, 0, 0)
    return pl.pallas_call(
        functools.partial(_inproj_kernel, steps=steps, s5_cols=s5_cols),
        grid=(m // tm,),
        in_specs=[
            pl.BlockSpec((tm, d), lambda i: (i, 0)),
            pl.BlockSpec((1, d), lambda i: (0, 0)),
            pl.BlockSpec((1, 1, d), vec),
            pl.BlockSpec((1, 1, d), vec),
            pl.BlockSpec((None, d, d_pad), lambda i: (layer, 0, 0), pipeline_mode=pl.Buffered(1)),
            pl.BlockSpec((None, w_s5, d), lambda i: (layer, 0, 0), pipeline_mode=pl.Buffered(1)),
        ],
        out_specs=[pl.BlockSpec((tm, d_out), lambda i: (i, 0)),
                   pl.BlockSpec((w_s5, SUBLANES, LANES), lambda i: (0, i // steps, 0))],
        out_shape=[jax.ShapeDtypeStruct((m, d_out), F32),
                   jax.ShapeDtypeStruct((w_s5, m // LANES, LANES), F32)],
        compiler_params=_params("arbitrary"),
        name="inproj",
    )(h, g.reshape(1, d), shift, scale, w, w_s5_t)


def _ret_tables(seq_len):
    c = RET_CHUNK
    pos = jnp.arange(seq_len, dtype=F32)
    inv_freq = ROPE_BASE ** (-jnp.arange(0, RET_DK, 2, dtype=F32) / RET_DK)
    ang = pos[:, None] * inv_freq[None, :]
    cos, sin = jnp.cos(ang), jnp.sin(ang)
    cs = jnp.concatenate([cos, cos], axis=-1)
    sn = jnp.concatenate([-sin, sin], axis=-1)
    lg = jnp.log1p(-jnp.exp2(-5.0 - jnp.arange(RET_HEADS, dtype=F32)))
    idx = jnp.arange(c, dtype=F32)
    dist = jnp.abs(idx[:, None] - idx[None, :])
    dmat = jnp.exp(dist[None] * lg[:, None, None])
    rows = jnp.stack([c - 1.0 - idx, idx, idx + 1.0, c - idx])
    dvec = jnp.exp(rows[None] * lg[:, None, None])
    dvec = jnp.broadcast_to(dvec[..., None], dvec.shape + (RET_DK,))
    gc = jnp.broadcast_to(jnp.exp(c * lg)[:, None, None], (RET_HEADS, 1, RET_DV))
    return cs, sn, dmat, dvec, gc


def _ret_kernel(q_ref, k_ref, v_ref, g_ref, cs_ref, sn_ref, dmat_ref, dvec_ref, gc_ref, o_ref,
                q_scr, k_scr, qf_scr, qb_scr, kf_scr, kb_scr, of_scr, ob_scr, sf_scr, sb_scr,
                *, nchunks):
    c = RET_CHUNK
    scale = RET_DK ** -0.5
    half = nchunks // 2

    def chunk_rows(n):
        return pl.ds(n * c, c) if isinstance(n, int) else pl.ds(pl.multiple_of(n * c, c), c)

    def prepare(n):
        rows = chunk_rows(n)
        cs = cs_ref[rows, :]
        sn = sn_ref[rows, :]
        q = q_ref[0, rows, :]
        k = k_ref[0, rows, :]
        q = (q * cs + pltpu.roll(q, RET_DK // 2, 1) * sn) * scale
        k = k * cs + pltpu.roll(k, RET_DK // 2, 1) * sn
        q_scr[rows, :] = q.astype(BF16)
        k_scr[rows, :] = k.astype(BF16)
        kf_scr[rows, :] = (k * dvec_ref[0, 0]).astype(BF16)
        kb_scr[rows, :] = (k * dvec_ref[0, 1]).astype(BF16)
        qf_scr[rows, :] = (q * dvec_ref[0, 2]).astype(BF16)
        qb_scr[rows, :] = (q * dvec_ref[0, 3]).astype(BF16)

    gc = gc_ref[0]

    def recur(n):
        rf = chunk_rows(n)
        vf = v_ref[0, rf, :].astype(BF16)
        scores = lax.dot_general(q_scr[rf, :], k_scr[rf, :], NT_DIMS,
                                 preferred_element_type=F32) * dmat_ref[0]
        s = sf_scr[...]
        o_fwd = (jnp.dot(scores.astype(BF16), vf, preferred_element_type=F32)
                 + jnp.dot(qf_scr[rf, :], s.astype(BF16), preferred_element_type=F32))
        sf_scr[...] = gc * s + lax.dot_general(kf_scr[rf, :], vf, TN_DIMS,
                                               preferred_element_type=F32)
        rb = chunk_rows(nchunks - 1 - n)
        vb = v_ref[0, rb, :].astype(BF16)
        s = sb_scr[...]
        o_bwd = jnp.dot(qb_scr[rb, :], s.astype(BF16), preferred_element_type=F32)
        sb_scr[...] = gc * s + lax.dot_general(kb_scr[rb, :], vb, TN_DIMS,
                                               preferred_element_type=F32)
        return o_fwd, o_bwd

    def finish(n, o):
        rows = chunk_rows(n)
        oc = o - jnp.mean(o, axis=-1, keepdims=True)
        ln = oc * lax.rsqrt(jnp.mean(oc * oc, axis=-1, keepdims=True) + EPS)
        g = g_ref[0, rows, :]
        o_ref[0, rows, :] = (g * _sigmoid(g) * ln).astype(o_ref.dtype)

    def first_half(n, carry, prepare_next=True):
        if prepare_next:
            prepare(n + 1)
            prepare(nchunks - 2 - n)
        o_fwd, o_bwd = recur(n)
        of_scr[chunk_rows(n), :] = o_fwd
        ob_scr[chunk_rows(nchunks - 1 - n), :] = o_bwd
        return carry

    def second_half(n, carry):
        o_fwd, o_bwd = recur(n)
        finish(n, o_fwd + ob_scr[chunk_rows(n), :])
        finish(nchunks - 1 - n, of_scr[chunk_rows(nchunks - 1 - n), :] + o_bwd)
        return carry

    sf_scr[...] = jnp.zeros_like(sf_scr)
    sb_scr[...] = jnp.zeros_like(sb_scr)
    prepare(0)
    prepare(nchunks - 1)
    lax.fori_loop(0, half - 1, first_half, 0, unroll=2)
    first_half(half - 1, 0, prepare_next=False)
    lax.fori_loop(half, nchunks, second_half, 0, unroll=2)


def _retention(proj, tables):
    bsz, seq_len, _ = proj.shape
    cs, sn, dmat, dvec, gc = tables
    assert seq_len % (2 * RET_CHUNK) == 0
    h = RET_HEADS
    kq, kk = 0, h
    kv, kg = (2 * h * RET_DK) // RET_DV, (2 * h * RET_DK + h * RET_DV) // RET_DV
    tok = lambda off: (lambda b, i: (b, 0, off + i))
    return pl.pallas_call(
        functools.partial(_ret_kernel, nchunks=seq_len // RET_CHUNK),
        grid=(bsz, h),
        in_specs=[
            pl.BlockSpec((1, seq_len, RET_DK), tok(kq)),
            pl.BlockSpec((1, seq_len, RET_DK), tok(kk)),
            pl.BlockSpec((1, seq_len, RET_DV), tok(kv)),
            pl.BlockSpec((1, seq_len, RET_DV), tok(kg)),
            pl.BlockSpec((seq_len, RET_DK), lambda b, i: (0, 0), pipeline_mode=pl.Buffered(1)),
            pl.BlockSpec((seq_len, RET_DK), lambda b, i: (0, 0), pipeline_mode=pl.Buffered(1)),
            pl.BlockSpec((1, RET_CHUNK, RET_CHUNK), lambda b, i: (i, 0, 0)),
            pl.BlockSpec((1, 4, RET_CHUNK, RET_DK), lambda b, i: (i, 0, 0, 0)),
            pl.BlockSpec((1, 1, RET_DV), lambda b, i: (i, 0, 0)),
        ],
        out_specs=pl.BlockSpec((1, seq_len, RET_DV), lambda b, i: (b, 0, i)),
        out_shape=jax.ShapeDtypeStruct((bsz, seq_len, h * RET_DV), BF16),
        scratch_shapes=(
            [pltpu.VMEM((seq_len, RET_DK), BF16)] * 6
            + [pltpu.VMEM((seq_len, RET_DV), F32)] * 2
            + [pltpu.VMEM((RET_DK, RET_DV), F32)] * 2
        ),
        compiler_params=_params("parallel", "parallel"),
        name="retention",
    )(proj, proj, proj, proj, cs, sn, dmat, dvec, gc)


def _gla_kernel(q_ref, k_ref, v_ref, g_ref, lr_ref, wf_ref, wb_ref, bf_ref, bb_ref, o_ref,
                q0_scr, q1_scr, kin_scr, kst_scr, dec_scr, o_scr, st_scr, *, nblocks):
    c, r = GLA_CHUNK, GLA_BLOCK
    per = r // c
    dk2, dv2 = 2 * GLA_DK, 2 * GLA_DV
    scale = GLA_DK ** -0.5
    block_rows = lambda m: pl.ds(pl.multiple_of(m * r, r), r)

    ri = lax.broadcasted_iota(jnp.int32, (r, r), 0)
    ci = lax.broadcasted_iota(jnp.int32, (r, r), 1)
    same_chunk = (ri // c) == (ci // c)
    lane = lax.broadcasted_iota(jnp.int32, (r, dk2), 1)
    v_lane = lax.broadcasted_iota(jnp.int32, (r, dv2), 1)
    se = lax.broadcasted_iota(jnp.int32, (dv2, dk2), 0)
    sd = lax.broadcasted_iota(jnp.int32, (dv2, dk2), 1)
    same_head = jnp.where(se < GLA_DV, jnp.where(sd < GLA_DK, 1.0, 0.0),
                          jnp.where(sd >= GLA_DK, 1.0, 0.0)).astype(F32)

    def direction(w_ref, b_ref, reverse):
        if reverse:
            cum = same_chunk & (ci >= ri)
            keep = same_chunk & (ci > ri)
        else:
            cum = same_chunk & (ci <= ri)
            keep = same_chunk & (ci <= ri)
        sum_mat = jnp.concatenate([jnp.where(cum, 1.0, 0.0), jnp.where(same_chunk, 1.0, 0.0)],
                                  axis=0).astype(BF16)

        def prepare(m, carry):
            rows = block_rows(m)
            lr = lr_ref[0, rows, :]
            lr_hi = lr.astype(BF16)
            lr_mid = (lr - lr_hi.astype(F32)).astype(BF16)
            x = jnp.dot(jnp.concatenate([lr_hi, lr_mid, lr_hi], axis=1), w_ref[0],
                        preferred_element_type=F32) + b_ref[0]
            la = (jnp.minimum(x, 0.0) - jnp.log(1.0 + jnp.exp(-jnp.abs(x)))) * (1.0 / GLA_TAU)
            hi = la.astype(BF16)
            rest = la - hi.astype(F32)
            mid = rest.astype(BF16)
            lo = (rest - mid.astype(F32)).astype(BF16)
            sums = jnp.dot(sum_mat, jnp.concatenate([hi, mid, lo], axis=1),
                           preferred_element_type=F32)
            sums = sums[:, :dk2] + sums[:, dk2:2 * dk2] + sums[:, 2 * dk2:]
            b, total = sums[:r], sums[r:]
            q_in = q_ref[0, rows, :] * scale * jnp.exp(b)
            k = k_ref[0, rows, :]
            q0_scr[rows, :] = jnp.where(lane < GLA_DK, q_in, 0.0).astype(BF16)
            q1_scr[rows, :] = jnp.where(lane >= GLA_DK, q_in, 0.0).astype(BF16)
            kin_scr[rows, :] = (k * jnp.exp(-b)).astype(BF16)
            kst_scr[rows, :] = (k * jnp.exp(total - b)).astype(BF16)
            dec_scr[rows, :] = jnp.exp(total)
            return carry

        lax.fori_loop(0, nblocks, prepare, 0, unroll=4)
        st_scr[...] = jnp.zeros_like(st_scr)

        def recur(t, carry):
            rows = block_rows(nblocks - 1 - t if reverse else t)
            q0 = q0_scr[rows, :]
            q1 = q1_scr[rows, :]
            kin = kin_scr[rows, :]
            kst = kst_scr[rows, :]
            dec = dec_scr[rows, :]
            vb = v_ref[0, rows, :].astype(BF16)
            scores = [jnp.where(keep, lax.dot_general(qh, kin, NT_DIMS, preferred_element_type=F32),
                                0.0).astype(BF16) for qh in (q0, q1)]
            v_heads = jnp.concatenate([jnp.where(v_lane < GLA_DV, vb, jnp.zeros_like(vb)),
                                       jnp.where(v_lane >= GLA_DV, vb, jnp.zeros_like(vb))], axis=0)
            intra = jnp.dot(jnp.concatenate(scores, axis=1), v_heads, preferred_element_type=F32)
            qin = q0 + q1
            st = st_scr[...]
            inter = [None] * per
            for i in (range(per - 1, -1, -1) if reverse else range(per)):
                sl = slice(i * c, (i + 1) * c)
                inter[i] = lax.dot_general(qin[sl], st.astype(BF16), NT_DIMS,
                                           preferred_element_type=F32)
                kv_t = lax.dot_general(vb[sl], kst[sl], TN_DIMS, preferred_element_type=F32)
                st = st * dec[i * c:i * c + 1, :] + kv_t * same_head
            st_scr[...] = st
            o = intra + jnp.concatenate(inter, axis=0)
            if not reverse:
                o_scr[rows, :] = o
            else:
                o = o + o_scr[rows, :]
                normed = []
                for hh in range(2):
                    oh = o[:, hh * GLA_DV:(hh + 1) * GLA_DV]
                    normed.append(oh * lax.rsqrt(jnp.mean(oh * oh, axis=-1, keepdims=True) + EPS))
                g = g_ref[0, rows, :]
                o_ref[0, rows, :] = (g * _sigmoid(g)
                                     * jnp.concatenate(normed, axis=1)).astype(o_ref.dtype)
            return carry

        lax.fori_loop(0, nblocks, recur, 0, unroll=4)

    direction(wf_ref, bf_ref, False)
    direction(wb_ref, bb_ref, True)


def _gla(proj, w_gate, b_gate, col0):
    bsz, seq_len, _ = proj.shape
    pairs = GLA_HEADS // 2
    dk2, dv2 = 2 * GLA_DK, 2 * GLA_DV
    c_q = col0
    c_k = c_q + GLA_HEADS * GLA_DK
    c_v = c_k + GLA_HEADS * GLA_DK
    c_g = c_v + GLA_HEADS * GLA_DV
    c_lr = c_g + GLA_HEADS * GLA_DV
    wg = w_gate.astype(F32).reshape(2, GLA_RANK, pairs, dk2).transpose(0, 2, 1, 3)
    wf = jnp.zeros((pairs, LANES, dk2), F32).at[:, :GLA_RANK].set(wg[0])
    wb = jnp.zeros((pairs, LANES, dk2), F32).at[:, GLA_RANK:2 * GLA_RANK].set(wg[1])

    def split3(w):
        w_hi = w.astype(BF16)
        w_mid = (w - w_hi.astype(F32)).astype(BF16)
        return jnp.concatenate([w_hi, w_hi, w_mid], axis=1)

    wf, wb = split3(wf), split3(wb)
    bg = b_gate.astype(F32).reshape(2, pairs, 1, dk2)
    tok = lambda col, width: (lambda b, p: (b, 0, col // width + p))
    mat = lambda b, p: (p, 0, 0)
    return pl.pallas_call(
        functools.partial(_gla_kernel, nblocks=seq_len // GLA_BLOCK),
        grid=(bsz, pairs),
        in_specs=[
            pl.BlockSpec((1, seq_len, dk2), tok(c_q, dk2)),
            pl.BlockSpec((1, seq_len, dk2), tok(c_k, dk2)),
            pl.BlockSpec((1, seq_len, dv2), tok(c_v, dv2)),
            pl.BlockSpec((1, seq_len, dv2), tok(c_g, dv2)),
            pl.BlockSpec((1, seq_len, LANES), lambda b, p: (b, 0, c_lr // LANES)),
            pl.BlockSpec((1, 3 * LANES, dk2), mat),
            pl.BlockSpec((1, 3 * LANES, dk2), mat),
            pl.BlockSpec((1, 1, dk2), mat),
            pl.BlockSpec((1, 1, dk2), mat),
        ],
        out_specs=pl.BlockSpec((1, seq_len, dv2), lambda b, p: (b, 0, p)),
        out_shape=jax.ShapeDtypeStruct((bsz, seq_len, GLA_HEADS * GLA_DV), BF16),
        scratch_shapes=(
            [pltpu.VMEM((seq_len, dk2), BF16)] * 4
            + [pltpu.VMEM((seq_len, dk2), F32), pltpu.VMEM((seq_len, dv2), F32),
               pltpu.VMEM((dv2, dk2), F32)]
        ),
        compiler_params=_params("parallel", "parallel"),
        name="gla",
    )(proj, proj, proj, proj, proj, wf, wb, bg[0], bg[1])


def _swap(x):
    return pltpu.roll(x, S5_STATE, 1)


def _s5_prep_kernel(lre_ref, lim_ref, ldt_ref, brr_ref, bis_ref, cr_ref, ci_ref,
                    m_ref, e_ref, ft_ref, a_ref):
    t_len = S5_CHUNK
    gsz = S5_GROUP
    width = gsz * t_len
    first = lax.broadcasted_iota(jnp.int32, (1, LANES), 1) < S5_STATE
    one_zero = jnp.where(first, 1.0, 0.0).astype(F32)
    sign = jnp.where(first, 1.0, -1.0).astype(F32)

    def packed_exp(t, are, aim):
        ph = t * aim
        return jnp.exp(t * are) * jnp.where(first, jnp.cos(ph), jnp.sin(ph))

    def dup_re(x):
        return jnp.where(first, x, _swap(x))

    def dup_im_signed(x):
        return jnp.where(first, -_swap(x), x)

    def per_channel(pw, w_a, w_b):
        return jnp.concatenate([pw * w_a[c:c + 1, :] + _swap(pw) * w_b[c:c + 1, :]
                                for c in range(gsz)], axis=0)

    trow = lax.broadcasted_iota(jnp.int32, (t_len, 1), 0).astype(F32)
    panels = []
    for d in range(2):
        lre = lre_ref[0, d:d + 1, :]
        lim = lim_ref[0, d:d + 1, :]
        dt = jnp.exp(ldt_ref[0, d:d + 1, :])
        are, aim = lre * dt, lim * dt
        num = packed_exp(jnp.ones((1, 1), F32), are, aim) - one_zero
        inv_den = 1.0 / (lre * lre + lim * lim)
        coef = num * (lre * inv_den) + _swap(num) * (lim * inv_den * sign)
        bbar = coef * brr_ref[0, d] + _swap(coef) * bis_ref[0, d]
        if d == 0:
            t_e, t_f, t_g = t_len - 1.0 - trow, trow + 1.0, trow
        else:
            t_e, t_f, t_g = trow, t_len - trow, t_len - 1.0 - trow
        e_mat = per_channel(packed_exp(t_e, are, aim), dup_re(bbar), dup_im_signed(bbar))
        e_ref[0, :, d * LANES:(d + 1) * LANES] = e_mat.astype(BF16)
        cr, ci = cr_ref[0, d], ci_ref[0, d]
        ft_ref[0, d] = per_channel(packed_exp(t_f, are, aim), cr, ci).astype(BF16)
        g_mat = per_channel(packed_exp(t_g, are, aim), cr, ci)
        panels.append(lax.dot_general(bbar, g_mat, NT_DIMS, preferred_element_type=F32,
                                      precision=HIGHEST))
        kcol = lax.broadcasted_iota(jnp.int32, (8, 1), 0)
        tk = (t_len * jnp.left_shift(1, kcol)).astype(F32)
        ak = packed_exp(tk, are, aim)
        a_ref[0, d, 0:8, :] = dup_re(ak)
        a_ref[0, d, 8:16, :] = dup_im_signed(ak)

    p_fwd, p_bwd = panels
    j = lax.broadcasted_iota(jnp.int32, (t_len, width), 0)
    pos = lax.broadcasted_iota(jnp.int32, (t_len, width), 1) % t_len
    for c in range(gsz):
        fwd = pltpu.roll(jnp.broadcast_to(p_fwd[c:c + 1, :], (t_len, width)), 0, 1,
                         stride=1, stride_axis=0)
        bwd = pltpu.roll(jnp.broadcast_to(p_bwd[c:c + 1, :], (t_len, width)), width - (t_len - 1), 1,
                         stride=1, stride_axis=0)
        blk = jnp.where(pos >= j, fwd, 0.0) + jnp.where(pos <= j, bwd, 0.0)
        m_ref[0, c * t_len:(c + 1) * t_len, :] = blk.astype(BF16)


def _s5_apply_kernel(u_ref, m_ref, e_ref, ft_ref, a_ref, d_ref, y_ref, *, blocks_per_seq, nsteps):
    t_len = S5_CHUNK
    per = LANES // t_len
    nblk = u_ref.shape[1]
    xc = [u_ref[c] for c in range(S5_GROUP)]
    ys, sums = [], []
    for q in range(per):
        u = jnp.concatenate([x[:, q * t_len:(q + 1) * t_len] for x in xc], axis=1)
        ub = u.astype(BF16)
        ys.append(jnp.dot(ub, m_ref[0], preferred_element_type=F32) + u * d_ref[0])
        sums.append(jnp.dot(ub, e_ref[0], preferred_element_type=F32))
    nidx = lax.broadcasted_iota(jnp.int32, (nblk, LANES), 0) % blocks_per_seq
    for d in range(2):

        def block_shift(val, sh):
            if d == 0:
                return jnp.where(nidx >= sh, pltpu.roll(val, sh, 0), 0.0)
            return jnp.where(nidx < blocks_per_seq - sh, pltpu.roll(val, nblk - sh, 0), 0.0)

        def chunk_shift(x, sh):
            if sh % per == 0:
                return [block_shift(v, sh // per) for v in x]
            src = [q - sh if d == 0 else q + sh for q in range(per)]
            return [x[s] if 0 <= s < per else block_shift(x[s % per], 1) for s in src]

        x = [s[:, d * LANES:(d + 1) * LANES] for s in sums]
        for k in range(nsteps):
            xs = chunk_shift(x, 1 << k)
            x = [v + a_ref[0, d, k:k + 1, :] * w + a_ref[0, d, 8 + k:9 + k, :] * _swap(w)
                 for v, w in zip(x, xs)]
        for q, v in enumerate(chunk_shift(x, 1)):
            ys[q] = ys[q] + lax.dot_general(v.astype(BF16), ft_ref[0, d], NT_DIMS,
                                            preferred_element_type=F32)
    for c in range(S5_GROUP):
        y_ref[c] = jnp.concatenate([y[:, c * t_len:(c + 1) * t_len] for y in ys], axis=1)


def _s5(su_t, seq_len, lam_re, lam_im, log_dt, b_re, b_im, c_re, c_im, d_skip):
    w, nblk, _ = su_t.shape
    g = w // S5_GROUP
    t_len = S5_CHUNK
    nchunks = seq_len // t_len
    nsteps = max(1, (nchunks - 1).bit_length())
    assert nsteps <= 8 and seq_len % LANES == 0
    width = S5_GROUP * t_len
    dup = lambda a: jnp.concatenate([a, a], axis=-1)
    gd = lambda a: jnp.moveaxis(a.astype(F32), 0, 1)
    lre = dup(gd(lam_re))
    lim = dup(gd(lam_im))
    ldt = jnp.broadcast_to(gd(log_dt)[..., None], (g, 2, LANES))
    bt_re = jnp.swapaxes(gd(b_re), -1, -2)
    bt_im = jnp.swapaxes(gd(b_im), -1, -2)
    brr = jnp.concatenate([bt_re, bt_re], axis=-1)
    bis = jnp.concatenate([-bt_im, bt_im], axis=-1)
    cre, cim = gd(c_re), gd(c_im)
    cr = jnp.concatenate([cre, -cre], axis=-1)
    ci = jnp.concatenate([-cim, -cim], axis=-1)
    vec3 = pl.BlockSpec((1, 2, LANES), lambda i: (i, 0, 0))
    mat4 = pl.BlockSpec((1, 2, S5_GROUP, LANES), lambda i: (i, 0, 0, 0))
    m_mat, e_mat, ft_mat, a_mat = pl.pallas_call(
        _s5_prep_kernel,
        grid=(g,),
        in_specs=[vec3, vec3, vec3, mat4, mat4, mat4, mat4],
        out_specs=[
            pl.BlockSpec((1, width, width), lambda i: (i, 0, 0)),
            pl.BlockSpec((1, width, 2 * LANES), lambda i: (i, 0, 0)),
            pl.BlockSpec((1, 2, width, LANES), lambda i: (i, 0, 0, 0)),
            pl.BlockSpec((1, 2, 16, LANES), lambda i: (i, 0, 0, 0)),
        ],
        out_shape=[
            jax.ShapeDtypeStruct((g, width, width), BF16),
            jax.ShapeDtypeStruct((g, width, 2 * LANES), BF16),
            jax.ShapeDtypeStruct((g, 2, width, LANES), BF16),
            jax.ShapeDtypeStruct((g, 2, 16, LANES), F32),
        ],
        compiler_params=_params("parallel"),
        name="s5_prep",
    )(lre, lim, ldt, brr, bis, cr, ci)
    d_rep = jnp.repeat(d_skip.astype(F32).reshape(g, 1, S5_GROUP), t_len, axis=-1)
    blk = pl.BlockSpec((S5_GROUP, nblk, LANES), lambda i: (i, 0, 0))
    return pl.pallas_call(
        functools.partial(_s5_apply_kernel, blocks_per_seq=seq_len // LANES, nsteps=nsteps),
        grid=(g,),
        in_specs=[
            blk,
            pl.BlockSpec((1, width, width), lambda i: (i, 0, 0)),
            pl.BlockSpec((1, width, 2 * LANES), lambda i: (i, 0, 0)),
            pl.BlockSpec((1, 2, width, LANES), lambda i: (i, 0, 0, 0)),
            pl.BlockSpec((1, 2, 16, LANES), lambda i: (i, 0, 0, 0)),
            pl.BlockSpec((1, 1, width), lambda i: (i, 0, 0)),
        ],
        out_specs=blk,
        out_shape=jax.ShapeDtypeStruct(su_t.shape, F32),
        compiler_params=_params("parallel"),
        name="s5_apply",
    )(su_t, m_mat, e_mat, ft_mat, a_mat, d_rep)


def _outproj_kernel(h_ref, gt_ref, yr_ref, ys_ref, yg_ref, wglu_ref, bglu_ref, wout_ref, o_ref,
                    *, steps):
    sub = o_ref.shape[0] // LANES
    base = (pl.program_id(0) % steps) * sub
    y = jnp.concatenate([ys_ref[:, pl.ds(base + k, 1), :][:, 0, :] for k in range(sub)], axis=1)
    z = y * (0.5 * (1.0 + jnp.tanh(math.sqrt(2.0 / math.pi) * (y + 0.044715 * (y * y * y)))))
    gl = jnp.dot(wglu_ref[...], z.astype(BF16), preferred_element_type=F32) + bglu_ref[...]
    s5 = (z * _sigmoid(gl)).astype(BF16)
    r_end = yr_ref.shape[1]
    s_end = r_end + s5.shape[0]
    acc = jnp.dot(yr_ref[...], wout_ref[:r_end, :], preferred_element_type=F32)
    acc = acc + jnp.dot(yg_ref[...], wout_ref[s_end:, :], preferred_element_type=F32)
    acc = acc + lax.dot_general(s5, wout_ref[r_end:s_end, :], TN_DIMS, preferred_element_type=F32)
    o_ref[...] = h_ref[...] + gt_ref[0] * acc


def _outproj(h, seq_len, gate, y_ret, y_s5_t, y_gla, wglu_t, b_glu, w_out, layer, *, tm=512):
    m, d = h.shape
    tm = min(tm, seq_len)
    per_batch = seq_len // tm
    steps = SUBLANES * LANES // tm
    w_s5 = y_s5_t.shape[0]
    row = lambda width: pl.BlockSpec((tm, width), lambda i: (i, 0))
    whole = lambda a: pl.BlockSpec((None,) + a.shape[1:], lambda i: (layer, 0, 0),
                                   pipeline_mode=pl.Buffered(1))
    bglu = b_glu.astype(F32).reshape(-1, w_s5, 1)
    return pl.pallas_call(
        functools.partial(_outproj_kernel, steps=steps),
        grid=(m // tm,),
        in_specs=[
            row(d),
            pl.BlockSpec((1, 1, d), lambda i: (i // per_batch, 0, 0)),
            row(y_ret.shape[1]),
            pl.BlockSpec((w_s5, SUBLANES, LANES), lambda i: (0, i // steps, 0)),
            row(y_gla.shape[1]),
            whole(wglu_t), whole(bglu), whole(w_out),
        ],
        out_specs=row(d),
        out_shape=jax.ShapeDtypeStruct((m, d), F32),
        compiler_params=_params("parallel"),
        name="outproj",
    )(h, gate, y_ret, y_s5_t, y_gla, wglu_t, bglu, w_out)


def kernel(x, c, w_ada, b_ada, g_ffn1, ffn1_w1, ffn1_w3, ffn1_w2, g_mix, w_in, s5_lam_re, s5_lam_im, s5_log_dt, s5_b_re, s5_b_im, s5_c_re, s5_c_im, s5_d, s5_w_glu, s5_b_glu, gla_w_gate, gla_b_gate, w_out, g_ffn2, ffn2_w1, ffn2_w3, ffn2_w2, g_final):
    bsz, seq_len, d = x.shape
    depth = w_ada.shape[0]
    mod = _ada_mod(c, w_ada, b_ada).reshape(depth, bsz, N_MOD, 1, d)
    tables = _ret_tables(seq_len)
    ret_w = RET_HEADS * (2 * RET_DK + 2 * RET_DV)
    s5_w = s5_d.shape[1]
    h = x.reshape(bsz * seq_len, d)
    ffn_f32 = {0: (ffn1_w1, ffn1_w3, ffn1_w2), 1: (ffn2_w1, ffn2_w3, ffn2_w2)}
    ffn_w = tuple(w[0].astype(BF16) for w in ffn_f32[0])
    s5_cols = (ret_w, ret_w + s5_w)
    w_s5_t = jnp.swapaxes(w_in[:, :, s5_cols[0]:s5_cols[1]], 1, 2).astype(BF16)
    w_in = jnp.pad(w_in.astype(BF16), ((0, 0), (0, 0), (0, -w_in.shape[2] % LANES)))
    w_out = w_out.astype(BF16)
    w_glu_t = jnp.swapaxes(s5_w_glu.astype(BF16), 1, 2)
    for l in range(depth):
        sh1, sc1, gt1, sh2, sc2, gt2, sh3, sc3, gt3 = (mod[l, :, i] for i in range(N_MOD))
        h, ffn_w = _ffn(h, seq_len, g_ffn1[l], sh1, sc1, gt1, ffn_w, ffn_f32[1] + (l,))
        proj, su_t = _inproj(h, seq_len, g_mix[l], sh2, sc2, w_in, w_s5_t, l, s5_cols)
        proj = proj.reshape(bsz, seq_len, -1)
        y_ret = _retention(proj, tables)
        y_s5_t = _s5(su_t, seq_len, s5_lam_re[l], s5_lam_im[l], s5_log_dt[l],
                     s5_b_re[l], s5_b_im[l], s5_c_re[l], s5_c_im[l], s5_d[l])
        y_gla = _gla(proj, gla_w_gate[l], gla_b_gate[l], ret_w)
        h = _outproj(h, seq_len, gt2, y_ret.reshape(bsz * seq_len, -1), y_s5_t,
                     y_gla.reshape(bsz * seq_len, -1), w_glu_t, s5_b_glu, w_out, l)
        last = l == depth - 1
        h, ffn_w = _ffn(h, seq_len, g_ffn2[l], sh3, sc3, gt3, ffn_w,
                        None if last else ffn_f32[0] + (l + 1,), g_final if last else None)
    return h.reshape(bsz, seq_len, d)
```
